```python
import math
import jax
import jax.numpy as jnp
from jax import lax
import numpy as np

D_MODEL = 1024
BATCH = 16
SEQ = 2048
DEPTH = 1
DEC_BATCH = 8
DEC_SEQ = 16
PAST_LEN = 1024

CHUNK = 64
D_MIX = D_MODEL
D_MLSTM = D_MIX // 2
H_M = 4
DH_M = D_MLSTM // H_M
D_ATT = D_MIX - D_MLSTM
H_A = 8
DH_A = D_ATT // H_A
H_IDX = 8
D_IDX = 64
TOPK_MAX = 256
N_BUCKETS = 32
MAX_DIST = 128
N_KEYS = 128
N_EXPERTS = N_KEYS * N_KEYS
PEER_HEADS = 8
D_PKEY = 256
PEER_TOPK = 16
PEER_BLOCK = 128
EPS = 1e-6
IN_SIZES = (D_MLSTM, D_MLSTM, D_MLSTM, D_MLSTM, H_M, H_M,
            D_ATT, D_ATT, D_ATT, H_IDX * D_IDX, D_IDX, H_IDX)
D_IN = sum(IN_SIZES)

kernel_name = 'hybrid_mlstm_dsa_peer_stream_step'


def _split_points():
    pts, acc = [], 0
    for s in IN_SIZES[:-1]:
        acc += s
        pts.append(acc)
    return pts


def rmsnorm(x, g):
    xf = x.astype(jnp.float32)
    y = xf * lax.rsqrt(jnp.mean(xf * xf, axis=-1, keepdims=True) + EPS)
    return (y * g.astype(jnp.float32)).astype(x.dtype)


def t5_bucket(rel):
    nb = N_BUCKETS // 2
    max_exact = nb // 2
    ret = jnp.where(rel > 0, nb, 0).astype(jnp.int32)
    n = jnp.abs(rel)
    nf = jnp.maximum(n, 1).astype(jnp.float32)
    large = max_exact + (jnp.log(nf / max_exact) / math.log(MAX_DIST / max_exact)
                         * (nb - max_exact)).astype(jnp.int32)
    large = jnp.minimum(large, nb - 1)
    return ret + jnp.where(n < max_exact, n, large)


def mlstm_chunk(state, q, k, v, ig, lf):
    C, n, m = state
    L = q.shape[1]
    bt = jnp.moveaxis(jnp.cumsum(lf, axis=1), 1, 2)
    igt = jnp.moveaxis(ig, 1, 2)
    causal = jnp.tril(jnp.ones((L, L), dtype=bool))
    dmat = jnp.where(causal, bt[..., :, None] - bt[..., None, :] + igt[..., None, :], -jnp.inf)
    inter = bt + m[..., None]
    m_t = jnp.maximum(inter, jnp.max(dmat, axis=-1))
    w_intra = jnp.exp(dmat - m_t[..., None])
    w_inter = jnp.exp(inter - m_t)
    s = jnp.einsum('blhd,bshd->bhls', q, k) * w_intra
    num = (jnp.einsum('bhls,bshd->blhd', s, v)
           + jnp.einsum('bhvk,blhk->blhv', C, q) * jnp.moveaxis(w_inter, 1, 2)[..., None])
    nq = jnp.sum(s, axis=-1) + w_inter * jnp.einsum('bhk,blhk->bhl', n, q)
    denom = jnp.maximum(jnp.abs(nq), jnp.exp(-m_t))
    h = num / jnp.moveaxis(denom, 1, 2)[..., None]
    m_new = m_t[..., -1]
    wk = jnp.exp(bt[..., -1:] - bt + igt - m_new[..., None])
    decay = jnp.exp(bt[..., -1] + m - m_new)
    C_new = decay[..., None, None] * C + jnp.einsum('bhs,bshv,bshk->bhvk', wk, v, k)
    n_new = decay[..., None] * n + jnp.einsum('bhs,bshk->bhk', wk, k)
    return h, (C_new, n_new, m_new)


def mlstm_mix(q, k, v, o, ipre, fpre, state, b_i, b_f, g_hn, blocked):
    B, L, _ = q.shape
    f32 = jnp.float32
    q = q.astype(f32).reshape(B, L, H_M, DH_M)
    k = k.astype(f32).reshape(B, L, H_M, DH_M) * (DH_M ** -0.5)
    v = v.astype(f32).reshape(B, L, H_M, DH_M)
    ig = ipre.astype(f32) + b_i.astype(f32)
    lf = jax.nn.log_sigmoid(fpre.astype(f32) + b_f.astype(f32))
    if blocked:
        nc = L // CHUNK

        def to_chunks(a):
            return jnp.moveaxis(a.reshape(B, nc, CHUNK, *a.shape[2:]), 1, 0)

        def step(carry, xs):
            h_c, carry = mlstm_chunk(carry, *xs)
            return carry, h_c

        state, h = lax.scan(step, state, (to_chunks(q), to_chunks(k), to_chunks(v),
                                          to_chunks(ig), to_chunks(lf)))
        h = jnp.moveaxis(h, 0, 1).reshape(B, L, H_M, DH_M)
    else:
        h, state = mlstm_chunk(state, q, k, v, ig, lf)
    h = h * lax.rsqrt(jnp.mean(h * h, axis=-1, keepdims=True) + EPS) * g_hn.astype(f32).reshape(H_M, DH_M)
    h = h.reshape(B, L, D_MLSTM) * jax.nn.sigmoid(o.astype(f32))
    return h, state


def dsa_attend(q, qi, w, q_pos, k_all, v_all, ki_all, rel_bias, topk):
    f32 = jnp.float32
    L = k_all.shape[1]
    limit = jnp.minimum((q_pos // CHUNK + 1) * CHUNK, L)
    adm = jnp.arange(L, dtype=jnp.int32)[None, :] < limit[:, None]
    rel = jax.nn.relu(jnp.einsum('bqhd,bld->bqhl', qi, ki_all).astype(f32) * (D_IDX ** -0.5))
    iscore = jnp.einsum('bqh,bqhl->bql', w.astype(f32), rel)
    iscore = jnp.where(adm[None], iscore, -jnp.inf)
    _, idx = lax.top_k(iscore, topk)
    valid = idx < limit[None, :, None]
    k_sel = jax.vmap(lambda kb, ib: kb[ib])(k_all, idx)
    v_sel = jax.vmap(lambda vb, ib: vb[ib])(v_all, idx)
    bias = jnp.moveaxis(rel_bias[t5_bucket(idx - q_pos[None, :, None])], -1, 1)
    logits = (jnp.einsum('bqhd,bqkhd->bhqk', q, k_sel).astype(f32) * (DH_A ** -0.5)
              + bias.astype(f32))
    p = jax.nn.softmax(jnp.where(valid[:, None], logits, -jnp.inf), axis=-1)
    return jnp.einsum('bhqk,bqkhd->bqhd', p.astype(v_sel.dtype), v_sel)


def dsa_prompt(q, qi, w, k, v, ki, rel_bias):
    B, S = q.shape[:2]
    nb = S // CHUNK
    topk = min(TOPK_MAX, S // 4)
    pos = jnp.arange(S, dtype=jnp.int32).reshape(nb, CHUNK)

    def blk(a):
        return jnp.moveaxis(a.reshape(B, nb, CHUNK, *a.shape[2:]), 1, 0)

    def one(xs):
        qb, qib, wb, pb = xs
        return dsa_attend(qb, qib, wb, pb, k, v, ki, rel_bias, topk)

    out = lax.map(one, (blk(q), blk(qi), blk(w), pos))
    return jnp.moveaxis(out, 0, 1).reshape(B, S, D_ATT)


def peer(x, w_pq, subkeys, pu, pv):
    f32 = jnp.float32
    T = x.shape[0]
    qp = (x @ w_pq).reshape(T, PEER_HEADS, 2, D_PKEY // 2)
    s = jnp.einsum('thcd,hcnd->thcn', qp, subkeys).astype(f32)
    s1, i1 = lax.top_k(s[:, :, 0], PEER_TOPK)
    s2, i2 = lax.top_k(s[:, :, 1], PEER_TOPK)
    cand = (s1[..., :, None] + s2[..., None, :]).reshape(T, PEER_HEADS, PEER_TOPK * PEER_TOPK)
    cidx = (i1[..., :, None] * N_KEYS + i2[..., None, :]).reshape(T, PEER_HEADS, PEER_TOPK * PEER_TOPK)
    top_s, sel = lax.top_k(cand, PEER_TOPK)
    eidx = jnp.take_along_axis(cidx, sel, axis=-1)
    g = jax.nn.softmax(top_s, axis=-1)
    u = pu[eidx]
    vv = pv[eidx]
    a = jax.nn.gelu(jnp.einsum('thkd,td->thk', u, x).astype(f32))
    return jnp.einsum('thk,thkd->td', (g * a).astype(vv.dtype), vv).astype(x.dtype)


def peer_tokens(x, w_pq, subkeys, pu, pv, blocked):
    if blocked:
        T = x.shape[0]
        out = lax.map(lambda xb: peer(xb, w_pq, subkeys, pu, pv),
                      x.reshape(T // PEER_BLOCK, PEER_BLOCK, x.shape[-1]))
        return out.reshape(T, x.shape[-1])
    return peer(x, w_pq, subkeys, pu, pv)


def layer_forward(x, mstate, past, norm_mix, w_in, b_i, b_f, g_hn, w_out,
                  norm_ffn, w_pq, subkeys, pu, pv, rel_bias):
    B, L, D = x.shape
    blocked = past is None
    xn = rmsnorm(x, norm_mix)
    (m_q, m_k, m_v, m_o, m_i, m_f, a_q, a_k, a_v, i_q, i_k, i_w) = jnp.split(
        jnp.einsum('bld,de->ble', xn, w_in), _split_points(), axis=-1)
    h_m, mstate = mlstm_mix(m_q, m_k, m_v, m_o, m_i, m_f, mstate, b_i, b_f, g_hn, blocked)
    q = a_q.reshape(B, L, H_A, DH_A)
    k = a_k.reshape(B, L, H_A, DH_A)
    v = a_v.reshape(B, L, H_A, DH_A)
    qi = i_q.reshape(B, L, H_IDX, D_IDX)
    w = i_w.astype(jnp.float32) * (H_IDX ** -0.5)
    if blocked:
        h_a = dsa_prompt(q, qi, w, k, v, i_k, rel_bias)
    else:
        pk, pvv, pki = past
        k_all = jnp.concatenate([pk, k.astype(pk.dtype)], axis=1)
        v_all = jnp.concatenate([pvv, v.astype(pvv.dtype)], axis=1)
        ki_all = jnp.concatenate([pki, i_k.astype(pki.dtype)], axis=1)
        q_pos = pk.shape[1] + jnp.arange(L, dtype=jnp.int32)
        topk = min(TOPK_MAX, k_all.shape[1] // 4)
        h_a = dsa_attend(q, qi, w, q_pos, k_all, v_all, ki_all, rel_bias, topk).reshape(B, L, D_ATT)
    mix = jnp.concatenate([h_m.astype(x.dtype), h_a.astype(x.dtype)], axis=-1)
    h = x + jnp.einsum('ble,ed->bld', mix, w_out)
    hn = rmsnorm(h, norm_ffn)
    y = h + peer_tokens(hn.reshape(B * L, D), w_pq, subkeys, pu, pv, blocked).reshape(B, L, D)
    return y, (k, v, i_k), mstate


def setup_inputs(seed: int = 0) -> dict:
    key = jax.random.key(seed)
    ks = jax.random.split(key, 24)
    f32 = jnp.float32

    def nrm(k, shape, s):
        return jax.random.normal(k, shape, f32) * s

    return {
        'x_prompt': nrm(ks[0], (BATCH, SEQ, D_MODEL), 1.0),
        'x_sample': nrm(ks[1], (DEC_BATCH, DEC_SEQ, D_MODEL), 1.0),
        'cache_k': nrm(ks[2], (DEPTH, DEC_BATCH, PAST_LEN, H_A, DH_A), 1.0),
        'cache_v': nrm(ks[3], (DEPTH, DEC_BATCH, PAST_LEN, H_A, DH_A), 1.0),
        'cache_kidx': nrm(ks[4], (DEPTH, DEC_BATCH, PAST_LEN, D_IDX), 1.0),
        'state_C': nrm(ks[5], (DEPTH, DEC_BATCH, H_M, DH_M, DH_M), 0.3),
        'state_n': nrm(ks[6], (DEPTH, DEC_BATCH, H_M, DH_M), 0.3),
        'state_m': nrm(ks[7], (DEPTH, DEC_BATCH, H_M), 0.5),
        'norm_mix': 1.0 + nrm(ks[8], (DEPTH, D_MODEL), 0.02),
        'w_in': nrm(ks[9], (DEPTH, D_MODEL, D_IN), D_MODEL ** -0.5),
        'b_igate': nrm(ks[10], (DEPTH, H_M), 0.1),
        'b_fgate': jnp.linspace(3.0, 6.0, H_M, dtype=f32)[None, :] + nrm(ks[11], (DEPTH, H_M), 0.1),
        'g_headnorm': 1.0 + nrm(ks[12], (DEPTH, D_MLSTM), 0.02),
        'w_out': nrm(ks[13], (DEPTH, D_MIX, D_MODEL), D_MIX ** -0.5),
        'norm_ffn': 1.0 + nrm(ks[14], (DEPTH, D_MODEL), 0.02),
        'w_peer_q': nrm(ks[15], (DEPTH, D_MODEL, PEER_HEADS * D_PKEY), D_MODEL ** -0.5),
        'peer_subkeys': nrm(ks[16], (DEPTH, PEER_HEADS, 2, N_KEYS, D_PKEY // 2), (D_PKEY // 2) ** -0.5),
        'peer_u': nrm(ks[17], (DEPTH, N_EXPERTS, D_MODEL), D_MODEL ** -0.5),
        'peer_v': nrm(ks[18], (DEPTH, N_EXPERTS, D_MODEL), 0.5),
        'rel_bias': nrm(ks[19], (N_BUCKETS, H_A), 0.5),
        'norm_final': 1.0 + nrm(ks[20], (D_MODEL,), 0.02),
    }


def reference(x_prompt, x_sample, cache_k, cache_v, cache_kidx, state_C, state_n, state_m,
              norm_mix, w_in, b_igate, b_fgate, g_headnorm, w_out, norm_ffn,
              w_peer_q, peer_subkeys, peer_u, peer_v, rel_bias, norm_final):
    f32 = jnp.float32
    y_p, y_s = x_prompt, x_sample
    pk, pv, pki, pC, pn, pm = [], [], [], [], [], []
    sk, sv, ski, sC, sn, sm = [], [], [], [], [], []
    for l in range(DEPTH):
        lw = (norm_mix[l], w_in[l], b_igate[l], b_fgate[l], g_headnorm[l], w_out[l],
              norm_ffn[l], w_peer_q[l], peer_subkeys[l], peer_u[l], peer_v[l], rel_bias)
        b = y_p.shape[0]
        zero = (jnp.zeros((b, H_M, DH_M, DH_M), f32), jnp.zeros((b, H_M, DH_M), f32),
                jnp.zeros((b, H_M), f32))
        y_p, (k_, v_, ki_), (c_, n_, m_) = layer_forward(y_p, zero, None, *lw)
        pk.append(k_)
        pv.append(v_)
        pki.append(ki_)
        pC.append(c_.astype(state_C.dtype))
        pn.append(n_.astype(state_n.dtype))
        pm.append(m_.astype(state_m.dtype))
        mst = (state_C[l].astype(f32), state_n[l].astype(f32), state_m[l].astype(f32))
        y_s, (k_, v_, ki_), (c_, n_, m_) = layer_forward(
            y_s, mst, (cache_k[l], cache_v[l], cache_kidx[l]), *lw)
        sk.append(k_)
        sv.append(v_)
        ski.append(ki_)
        sC.append(c_.astype(state_C.dtype))
        sn.append(n_.astype(state_n.dtype))
        sm.append(m_.astype(state_m.dtype))
    y_prompt = rmsnorm(y_p, norm_final)
    y_sample = rmsnorm(y_s, norm_final)
    new_k_prompt = jnp.stack(pk)
    new_v_prompt = jnp.stack(pv)
    new_kidx_prompt = jnp.stack(pki)
    new_C_prompt = jnp.stack(pC)
    new_n_prompt = jnp.stack(pn)
    new_m_prompt = jnp.stack(pm)
    new_k_sample = jnp.stack(sk)
    new_v_sample = jnp.stack(sv)
    new_kidx_sample = jnp.stack(ski)
    new_C_sample = jnp.stack(sC)
    new_n_sample = jnp.stack(sn)
    new_m_sample = jnp.stack(sm)
    return (y_prompt, y_sample, new_k_prompt, new_v_prompt, new_kidx_prompt,
            new_C_prompt, new_n_prompt, new_m_prompt,
            new_k_sample, new_v_sample, new_kidx_sample,
            new_C_sample, new_n_sample, new_m_sample)
```

```python
import functools
import math

import jax
import jax.numpy as jnp
from jax import lax
from jax.experimental import pallas as pl
from jax.experimental.pallas import tpu as pltpu

F32 = jnp.float32
BF16 = jnp.bfloat16
I32 = jnp.int32

CHUNK = 64
H_M = 4
DH_M = 128
D_MLSTM = H_M * DH_M
H_A = 8
DH_A = 64
D_ATT = H_A * DH_A
H_IDX = 8
D_IDX = 64
TOPK_MAX = 256
N_BUCKETS = 32
N_KEYS = 128
PEER_HEADS = 8
PEER_TOPK = 16
EPS = 1e-6

LANES = 128
KEY_TILE = 128
VMEM_LIMIT = 56 * 1024 * 1024
INT_MIN = -(2 ** 31)

_NT = (((1,), (1,)), ((), ()))
_TN = (((0,), (0,)), ((), ()))


def _nt(a, b):
    return lax.dot_general(a, b, _NT, preferred_element_type=F32)


def _params(sem, vmem=VMEM_LIMIT):
    return pltpu.CompilerParams(dimension_semantics=sem, vmem_limit_bytes=vmem)


def _inproj_kernel(x_ref, g_ref, w_ref, wvt_ref,
                   mq_ref, mk_ref, mv_ref, mo_ref, aq_ref, ak_ref, akb_ref, av_ref,
                   iq_ref, ik_ref, ikd_ref, gates_ref, vt_ref):
    x = x_ref[...]
    xn = x * lax.rsqrt(jnp.mean(x * x, axis=-1, keepdims=True) + EPS) * g_ref[...]
    xb = xn.astype(BF16)

    def proj(lo, hi):
        return jnp.dot(xb, w_ref[:, lo:hi], preferred_element_type=F32)

    mq_ref[...] = proj(0, 512).astype(BF16)
    mk_ref[...] = proj(512, 1024).astype(BF16)
    mv_ref[...] = proj(1024, 1536).astype(BF16)
    mo_ref[...] = proj(1536, 2048)
    aq_ref[...] = proj(2048, 2560).astype(BF16)
    ak = proj(2560, 3072)
    ak_ref[...] = ak
    akb_ref[...] = ak.astype(BF16)
    av_ref[...] = proj(3072, 3584)
    iq_ref[...] = proj(3584, 4096).astype(BF16)
    ikd = proj(4096, 4224)
    ik_ref[...] = ikd[:, :D_IDX]
    ikd_ref[...] = ikd.astype(BF16)
    gates_ref[...] = proj(4224, 4352)
    vt = _nt(wvt_ref[...], xb)
    for i in range(vt.shape[1] // KEY_TILE):
        vt_ref[i] = vt[:, i * KEY_TILE:(i + 1) * KEY_TILE].astype(BF16)


def _split_w_in(w_in):
    sizes = (D_MLSTM, D_MLSTM, D_MLSTM, D_MLSTM, H_M, H_M, D_ATT, D_ATT, D_ATT, H_IDX * D_IDX, D_IDX, H_IDX)
    pts, acc = [], 0
    for s in sizes[:-1]:
        acc += s
        pts.append(acc)
    return jnp.split(w_in, pts, axis=1)


def _inproj(x2, norm_g, w_in, tm):
    t, d = x2.shape
    m_q, m_k, m_v, m_o, m_i, m_f, a_q, a_k, a_v, i_q, i_k, i_w = _split_w_in(w_in)
    gates = jnp.concatenate([m_i, m_f, i_w], axis=1)
    gates = jnp.pad(gates, ((0, 0), (0, LANES - gates.shape[1])))
    w_main = jnp.concatenate([m_q, m_k, m_v, m_o, a_q, a_k, a_v, i_q, i_k, i_k, gates], axis=1).astype(BF16)
    w_vt = a_v.T.astype(BF16)
    nw = w_main.shape[1]
    row = lambda n: pl.BlockSpec((tm, n), lambda i: (i, 0))
    const = lambda shape: pl.BlockSpec(shape, lambda i: (0,) * len(shape))
    sds = jax.ShapeDtypeStruct
    out_shape = (
        sds((t, 512), BF16), sds((t, 512), BF16), sds((t, 512), BF16), sds((t, 512), F32),
        sds((t, 512), BF16), sds((t, 512), F32), sds((t, 512), BF16), sds((t, 512), F32),
        sds((t, 512), BF16), sds((t, D_IDX), F32), sds((t, LANES), BF16), sds((t, LANES), F32),
        sds((t // KEY_TILE, 512, KEY_TILE), BF16),
    )
    out_specs = (
        row(512), row(512), row(512), row(512), row(512), row(512), row(512), row(512),
        row(512), row(D_IDX), row(LANES), row(LANES),
        pl.BlockSpec((tm // KEY_TILE, 512, KEY_TILE), lambda i: (i, 0, 0)),
    )
    return pl.pallas_call(
        _inproj_kernel,
        grid=(t // tm,),
        in_specs=[row(d), const((1, d)), const((d, nw)), const((512, d))],
        out_specs=out_specs,
        out_shape=out_shape,
        compiler_params=_params(("parallel",)),
        name="inproj",
    )(x2, norm_g.reshape(1, d), w_main, w_vt)


def _mlstm_kernel(bi_ref, bf_ref, q_ref, k_ref, v_ref, o_ref, g_ref, ghn_ref, c0_ref, n0_ref, m0_ref,
                  h_ref, cout_ref, nout_ref, mout_ref, c_s, n_s, m_s, *, lc):
    c = pl.program_id(1)

    @pl.when(c == 0)
    def _():
        c_s[...] = c0_ref[0]
        n_s[...] = n0_ref[0]
        m_s[...] = m0_ref[0]

    g = g_ref[...]
    gt = g[:, 0:16].T
    ri = lax.broadcasted_iota(I32, (lc, lc), 0)
    ci = lax.broadcasted_iota(I32, (lc, lc), 1)
    causal = ci <= ri
    scale = DH_M ** -0.5

    for h in range(H_M):
        hs = slice(h * DH_M, (h + 1) * DH_M)
        ig_col = g[:, h:h + 1] + bi_ref[h]
        lf_col = jax.nn.log_sigmoid(g[:, H_M + h:H_M + h + 1] + bf_ref[h])
        ig_row = gt[h:h + 1, :] + bi_ref[h]
        lf_row = jax.nn.log_sigmoid(gt[H_M + h:H_M + h + 1, :] + bf_ref[h])
        bt_col = jnp.sum(jnp.where(causal, lf_row, 0.0), axis=1, keepdims=True)
        bt_row = jnp.sum(jnp.where(ri <= ci, lf_col, 0.0), axis=0, keepdims=True)
        dmat = jnp.where(causal, bt_col - bt_row + ig_row, -jnp.inf)
        m_old = m_s[h:h + 1, 0:1]
        inter = bt_col + m_old
        m_t = jnp.maximum(inter, jnp.max(dmat, axis=1, keepdims=True))
        w_intra = jnp.exp(dmat - m_t)
        w_inter = jnp.exp(inter - m_t)
        qh = q_ref[:, hs]
        kh = k_ref[:, hs]
        vh = v_ref[:, hs]
        s = _nt(qh, kh) * scale * w_intra
        cmat = c_s[h]
        n_row = n_s[h:h + 1, :]
        qc = _nt(qh, cmat.astype(BF16))
        num = jnp.dot(s.astype(BF16), vh, preferred_element_type=F32) + qc * w_inter
        qn = jnp.sum(qh.astype(F32) * n_row, axis=1, keepdims=True)
        nq = jnp.sum(s, axis=1, keepdims=True) + w_inter * qn
        denom = jnp.maximum(jnp.abs(nq), jnp.exp(-m_t))
        hh = num / denom
        m_new = m_t[lc - 1:lc, :]
        bt_last = bt_col[lc - 1:lc, :]
        wk_col = jnp.exp(bt_last - bt_col + ig_col - m_new)
        decay = jnp.exp(bt_last + m_old - m_new)
        wkv = (wk_col * vh.astype(F32)).astype(BF16)
        c_s[h] = decay * cmat + lax.dot_general(wkv, kh, _TN, preferred_element_type=F32) * scale
        n_s[h:h + 1, :] = decay * n_row + jnp.sum(wk_col * kh.astype(F32), axis=0, keepdims=True) * scale
        m_s[h:h + 1, :] = jnp.broadcast_to(m_new, (1, LANES))
        hn = hh * lax.rsqrt(jnp.mean(hh * hh, axis=1, keepdims=True) + EPS) * ghn_ref[:, hs]
        h_ref[:, hs] = (hn * jax.nn.sigmoid(o_ref[:, hs])).astype(BF16)

    @pl.when(c == pl.num_programs(1) - 1)
    def _():
        cout_ref[0] = c_s[...]
        nout_ref[0] = n_s[...]
        mout_ref[0] = m_s[...]


def _mlstm(mq, mk, mv, mo, gates, b_i, b_f, g_hn, c0, n0, m0, batch, lc):
    t = mq.shape[0]
    nc = t // (batch * lc)
    row = lambda n: pl.BlockSpec((lc, n), lambda b, c: (b * nc + c, 0))
    smem = pl.BlockSpec(memory_space=pltpu.SMEM)
    st4 = pl.BlockSpec((1, H_M, DH_M, DH_M), lambda b, c: (b, 0, 0, 0))
    st3 = pl.BlockSpec((1, 8, LANES), lambda b, c: (b, 0, 0))
    n0p = jnp.pad(n0, ((0, 0), (0, 8 - H_M), (0, 0)))
    m0p = jnp.broadcast_to(jnp.pad(m0, ((0, 0), (0, 8 - H_M)))[:, :, None], (batch, 8, LANES))
    sds = jax.ShapeDtypeStruct
    h, c_new, n_new, m_new = pl.pallas_call(
        functools.partial(_mlstm_kernel, lc=lc),
        grid=(batch, nc),
        in_specs=[smem, smem, row(512), row(512), row(512), row(512), row(LANES),
                  pl.BlockSpec((1, 512), lambda b, c: (0, 0)), st4, st3, st3],
        out_specs=(row(512), st4, st3, st3),
        out_shape=(sds((t, 512), BF16), sds((batch, H_M, DH_M, DH_M), F32),
                   sds((batch, 8, LANES), F32), sds((batch, 8, LANES), F32)),
        scratch_shapes=[pltpu.VMEM((H_M, DH_M, DH_M), F32), pltpu.VMEM((8, LANES), F32),
                        pltpu.VMEM((8, LANES), F32)],
        compiler_params=_params(("parallel", "arbitrary")),
        name="mlstm",
    )(b_i, b_f, mq, mk, mv, mo, gates, g_hn.reshape(1, 512), c0, n0p, m0p)
    return h, c_new, n_new[:, :H_M, :], m_new[:, :H_M, 0]


def _bias_kernel(rb_ref, nb_ref):
    kk = lax.broadcasted_iota(I32, (KEY_TILE, LANES), 0)
    qq = lax.broadcasted_iota(I32, (KEY_TILE, LANES), 1)
    nb = N_BUCKETS // 2
    max_exact = nb // 2
    for d in range(2):
        rel = kk - qq - KEY_TILE * d
        n = jnp.abs(rel)
        e = (pltpu.bitcast((n * n).astype(F32), I32) >> 23) - 127
        large = jnp.minimum(max_exact + (e - 6), nb - 1)
        bucket = jnp.where(rel > 0, nb, 0) + jnp.where(n < max_exact, n, large)
        for h in range(H_A):
            acc = jnp.zeros((KEY_TILE, LANES), F32)
            for b in range(N_BUCKETS):
                acc = jnp.where(bucket == b, rb_ref[b, h], acc)
            nb_ref[h, d] = acc


def _bias_tiles(rel_bias):
    return pl.pallas_call(
        _bias_kernel,
        in_specs=[pl.BlockSpec(memory_space=pltpu.SMEM)],
        out_shape=jax.ShapeDtypeStruct((H_A, 2, KEY_TILE, LANES), F32),
        name="bias_tiles",
    )(rel_bias)


def _dsa_kernel(rb_ref, q_ref, qi_ref, g_ref, k_ref, vt_ref, ki_ref, nb_ref, o_ref,
                keys_s, mb_s, lg_s, ot_s, *, nq, q_off, ltot, topk):
    j = pl.program_id(1)
    qbase = j * nq + q_off
    lim_max = jnp.minimum(((qbase + nq - 1) // CHUNK + 1) * CHUNK, ltot)
    nt = (lim_max + KEY_TILE - 1) // KEY_TILE
    tdiag = qbase // KEY_TILE
    qpos = qbase + lax.broadcasted_iota(I32, (1, nq), 1)
    limit_q = jnp.minimum((qpos // CHUNK + 1) * CHUNK, ltot)
    rowi = lax.broadcasted_iota(I32, (KEY_TILE, nq), 0)
    w_t = g_ref[:, 8:16].T * (H_IDX ** -0.5)
    lo_half = lax.broadcasted_iota(I32, (nq, LANES), 1) < DH_A

    def head_masked(ref, h):
        pair = ref[:, (h // 2) * LANES:(h // 2 + 1) * LANES]
        keep = lo_half if h % 2 == 0 else jnp.logical_not(lo_half)
        return jnp.where(keep, pair, jnp.zeros_like(pair))

    qi_m = [head_masked(qi_ref, h) for h in range(H_IDX)]

    def score_tile(t, carry):
        kid = ki_ref[pl.ds(pl.multiple_of(t * KEY_TILE, KEY_TILE), KEY_TILE), :]
        isc = jnp.zeros((KEY_TILE, nq), F32)
        for h in range(H_IDX):
            r = _nt(kid, qi_m[h])
            isc = isc + w_t[h:h + 1, :] * jnp.maximum(r * (D_IDX ** -0.5), 0.0)
        isc = jnp.where(isc == 0.0, 0.0, isc)
        adm = (t * KEY_TILE + rowi) < limit_q
        isc = jnp.where(adm, isc, -jnp.inf)
        bits = pltpu.bitcast(isc, I32)
        keys_s[t] = jnp.where(bits < 0, bits ^ 0x7FFFFFFF, bits)
        return carry

    lax.fori_loop(0, nt, score_tile, 0)

    def count(pred, cand):
        def body(t, acc):
            return acc + jnp.where(pred(keys_s[t], cand), 1, 0)
        acc = lax.fori_loop(0, nt, body, jnp.zeros((KEY_TILE, nq), I32))
        return jnp.sum(acc, axis=0, keepdims=True)

    ge = lambda a, b: a >= b
    zero = jnp.zeros((1, nq), I32)
    cur = jnp.where(count(ge, zero) >= topk, zero, jnp.full((1, nq), INT_MIN, I32))

    def bit_body(i, cur):
        cand = cur + jnp.left_shift(jnp.int32(1), 30 - i)
        return jnp.where(count(ge, cand) >= topk, cand, cur)

    thr = lax.fori_loop(0, 31, bit_body, cur)
    need = (topk - count(lambda a, b: a > b, thr)).astype(F32)
    tri = (lax.broadcasted_iota(I32, (KEY_TILE, KEY_TILE), 1)
           <= lax.broadcasted_iota(I32, (KEY_TILE, KEY_TILE), 0)).astype(BF16)

    def select_tile(t, before):
        key = keys_s[t]
        tie = jnp.where(key == thr, 1.0, 0.0)
        rank = jnp.dot(tri, tie.astype(BF16), preferred_element_type=F32) + before
        take = jnp.where(key > thr, 1.0, jnp.where(rank <= need, tie, 0.0))
        adm = (t * KEY_TILE + rowi) < limit_q
        mb_s[t] = jnp.where(adm, jnp.where(take > 0.0, 0.0, -jnp.inf), -jnp.inf)
        return before + jnp.sum(tie, axis=0, keepdims=True)

    lax.fori_loop(0, nt, select_tile, jnp.zeros((1, nq), F32))

    for h in range(H_A):
        q_m = head_masked(q_ref, h)
        pair = h // 2
        far = rb_ref[N_BUCKETS // 2 - 1, h]
        nb0 = nb_ref[h, 0]
        nb1 = nb_ref[h, 1]

        def logit_tile(t, mx):
            kt = k_ref[pl.ds(pl.multiple_of(t * KEY_TILE, KEY_TILE), KEY_TILE), pair * LANES:(pair + 1) * LANES]
            bias = jnp.where(t == tdiag, nb0, jnp.where(t == tdiag - 1, nb1, far))
            lg = _nt(kt, q_m) * (DH_A ** -0.5) + bias + mb_s[t]
            lg_s[t] = lg
            return jnp.maximum(mx, lg)

        mx = lax.fori_loop(0, nt, logit_tile, jnp.full((KEY_TILE, nq), -jnp.inf, F32))
        m = jnp.max(mx, axis=0, keepdims=True)

        def pv_tile(t, carry):
            l, acc = carry
            p = jnp.exp(lg_s[t] - m)
            vt = vt_ref[t, h * DH_A:(h + 1) * DH_A, :]
            return l + p, acc + jnp.dot(vt, p.astype(BF16), preferred_element_type=F32)

        l, acc = lax.fori_loop(0, nt, pv_tile, (jnp.zeros((KEY_TILE, nq), F32), jnp.zeros((DH_A, nq), F32)))
        ot_s[h * DH_A:(h + 1) * DH_A, :] = acc / jnp.sum(l, axis=0, keepdims=True)

    o_ref[...] = ot_s[...].T.astype(BF16)


def _dsa(rel_bias, nb, q, qi, gates, k, vt3, kid, *, batch, nq, q_off, lk, ltot, topk):
    t = q.shape[0]
    nqb = t // (batch * nq)
    ntile = lk // KEY_TILE
    qrow = lambda n: pl.BlockSpec((nq, n), lambda b, j: (b * nqb + j, 0))
    return pl.pallas_call(
        functools.partial(_dsa_kernel, nq=nq, q_off=q_off, ltot=ltot, topk=topk),
        grid=(batch, nqb),
        in_specs=[pl.BlockSpec(memory_space=pltpu.SMEM), qrow(512), qrow(512), qrow(LANES),
                  pl.BlockSpec((lk, 512), lambda b, j: (b, 0)),
                  pl.BlockSpec((ntile, 512, KEY_TILE), lambda b, j: (b, 0, 0)),
                  pl.BlockSpec((lk, LANES), lambda b, j: (b, 0)),
                  pl.BlockSpec((H_A, 2, KEY_TILE, nq), lambda b, j: (0, 0, 0, 0))],
        out_specs=qrow(512),
        out_shape=jax.ShapeDtypeStruct((t, 512), BF16),
        scratch_shapes=[pltpu.VMEM((ntile, KEY_TILE, nq), I32), pltpu.VMEM((ntile, KEY_TILE, nq), F32),
                        pltpu.VMEM((ntile, KEY_TILE, nq), F32), pltpu.VMEM((D_ATT, nq), F32)],
        compiler_params=_params(("parallel", "arbitrary")),
        name="dsa",
    )(rel_bias, q, qi, gates, k, vt3, kid, nb)


def _topk_rows(s, k, n):
    rows = lax.broadcasted_iota(I32, s.shape, 0)
    vals, idxs = [], []
    for _ in range(k):
        m = jnp.max(s, axis=0, keepdims=True)
        first = jnp.min(jnp.where(s == m, rows, n), axis=0, keepdims=True)
        vals.append(m)
        idxs.append(first)
        s = jnp.where(rows == first, -jnp.inf, s)
    return jnp.concatenate(vals, axis=0), jnp.concatenate(idxs, axis=0)


def _route_kernel(x_ref, hm_ref, ha_ref, wo_ref, g_ref, wpq_ref, sk_ref,
                  h_ref, hn_ref, eidx_ref, gate_ref, qp_s):
    tm = x_ref.shape[0]
    h = (x_ref[...]
         + jnp.dot(hm_ref[...], wo_ref[0:D_MLSTM, :], preferred_element_type=F32)
         + jnp.dot(ha_ref[...], wo_ref[D_MLSTM:, :], preferred_element_type=F32))
    h_ref[...] = h
    hn = h * lax.rsqrt(jnp.mean(h * h, axis=-1, keepdims=True) + EPS) * g_ref[...]
    hn_ref[...] = hn
    qp = jnp.dot(hn.astype(BF16), wpq_ref[...], preferred_element_type=F32)
    for grp in range(2 * PEER_HEADS):
        qp_s[grp] = qp[:, grp * N_KEYS:(grp + 1) * N_KEYS].astype(BF16)

    def head_body(hh, carry):
        s1 = _nt(sk_ref[2 * hh], qp_s[2 * hh])
        s2 = _nt(sk_ref[2 * hh + 1], qp_s[2 * hh + 1])
        v1, i1 = _topk_rows(s1, PEER_TOPK, N_KEYS)
        v2, i2 = _topk_rows(s2, PEER_TOPK, N_KEYS)
        cand = jnp.concatenate([v1[a:a + 1, :] + v2 for a in range(PEER_TOPK)], axis=0)
        cidx = jnp.concatenate([i1[a:a + 1, :] * N_KEYS + i2 for a in range(PEER_TOPK)], axis=0)
        rows = lax.broadcasted_iota(I32, cand.shape, 0)
        tops, eids = [], []
        for _ in range(PEER_TOPK):
            m = jnp.max(cand, axis=0, keepdims=True)
            first = jnp.min(jnp.where(cand == m, rows, PEER_TOPK * PEER_TOPK), axis=0, keepdims=True)
            hit = rows == first
            tops.append(m)
            eids.append(jnp.sum(jnp.where(hit, cidx, 0), axis=0, keepdims=True))
            cand = jnp.where(hit, -jnp.inf, cand)
        top_s = jnp.concatenate(tops, axis=0)
        ex = jnp.exp(top_s - top_s[0:1, :])
        off = pl.multiple_of(hh * PEER_TOPK, PEER_TOPK)
        gate_ref[pl.ds(off, PEER_TOPK), :] = ex / jnp.sum(ex, axis=0, keepdims=True)
        eidx_ref[pl.ds(off, PEER_TOPK), :] = jnp.concatenate(eids, axis=0)
        return carry

    lax.fori_loop(0, PEER_HEADS, head_body, 0)


def _route(x2, hm, ha, w_out, norm_g, w_pq, subkeys, tm):
    t, d = x2.shape
    npq = w_pq.shape[1]
    row = lambda n: pl.BlockSpec((tm, n), lambda i: (i, 0))
    col = lambda n: pl.BlockSpec((n, tm), lambda i: (0, i))
    const = lambda shape: pl.BlockSpec(shape, lambda i: (0,) * len(shape))
    sds = jax.ShapeDtypeStruct
    npick = PEER_HEADS * PEER_TOPK
    return pl.pallas_call(
        _route_kernel,
        grid=(t // tm,),
        in_specs=[row(d), row(D_MLSTM), row(D_ATT), const((D_MLSTM + D_ATT, d)), const((1, d)),
                  const((d, npq)), const((2 * PEER_HEADS, N_KEYS, N_KEYS))],
        out_specs=(row(d), row(d), col(npick), col(npick)),
        out_shape=(sds((t, d), F32), sds((t, d), F32), sds((npick, t), I32), sds((npick, t), F32)),
        scratch_shapes=[pltpu.VMEM((2 * PEER_HEADS, tm, N_KEYS), BF16)],
        compiler_params=_params(("parallel",)),
        name="route",
    )(x2, hm, ha, w_out.astype(BF16), norm_g.reshape(1, d), w_pq.astype(BF16),
      subkeys.reshape(2 * PEER_HEADS, N_KEYS, N_KEYS).astype(BF16))


def _pack_table(tab):
    e, d = tab.shape
    b = lax.bitcast_convert_type(tab.astype(BF16), jnp.uint16).astype(jnp.uint32)
    lo = b[:, :d // 2]
    hi = b[:, d // 2:]
    return ((hi << 16) | lo).reshape(e, d // 2 // LANES, LANES)


def _unpack_row(row):
    lo = pltpu.bitcast(row << 16, F32)
    hi = pltpu.bitcast(row & jnp.uint32(0xFFFF0000), F32)
    return lo, hi


def _gelu_tanh(x):
    return 0.5 * x * (1.0 + jnp.tanh(math.sqrt(2.0 / math.pi) * (x + 0.044715 * (x * x * x))))


def _peer_u_kernel(idx_ref, tab_ref, x_ref, gate_ref, c_ref, slab_s, a_s, *, npick):
    tm = x_ref.shape[0]
    ones = jnp.ones((8, LANES), BF16)

    def token(t, carry):
        xlo = x_ref[t, 0:4]
        xhi = x_ref[t, 4:8]
        base = t * npick
        for k in range(npick):
            lo, hi = _unpack_row(tab_ref[idx_ref[0, 0, base + k]])
            slab_s[k:k + 1, :] = jnp.sum(lo * xlo + hi * xhi, axis=0, keepdims=True)
        part = slab_s[...]
        p_hi = part.astype(BF16)
        p_lo = (part - p_hi.astype(F32)).astype(BF16)
        a = _nt(ones, p_hi) + _nt(ones, p_lo)
        a_s[pl.ds(t, 1), :] = a[0:1, :]
        return carry

    lax.fori_loop(0, tm, token, 0)
    c_ref[...] = gate_ref[...] * _gelu_tanh(a_s[...])


def _peer_v_kernel(idx_ref, c_ref, tab_ref, h_ref, y_ref, *, npick):
    tm = h_ref.shape[0]

    def token(t, carry):
        base = t * npick
        acc_lo = [jnp.zeros((4, LANES), F32) for _ in range(2)]
        acc_hi = [jnp.zeros((4, LANES), F32) for _ in range(2)]
        for k in range(npick):
            lo, hi = _unpack_row(tab_ref[idx_ref[0, 0, base + k]])
            c = c_ref[0, 0, base + k]
            acc_lo[k % 2] = acc_lo[k % 2] + c * lo
            acc_hi[k % 2] = acc_hi[k % 2] + c * hi
        y_ref[t, 0:4] = h_ref[t, 0:4] + (acc_lo[0] + acc_lo[1])
        y_ref[t, 4:8] = h_ref[t, 4:8] + (acc_hi[0] + acc_hi[1])
        return carry

    lax.fori_loop(0, tm, token, 0)


def _peer(hres, hn, eidx_t, gate_t, pu_packed, pv_packed, tm):
    t, d = hres.shape
    npick = eidx_t.shape[0]
    nblk = t // tm
    idx = eidx_t.T.reshape(nblk, 1, tm * npick)
    gate = gate_t.T
    smem_blk = pl.BlockSpec((1, 1, tm * npick), lambda i: (i, 0, 0), memory_space=pltpu.SMEM)
    tab_spec = pl.BlockSpec(pu_packed.shape, lambda i: (0, 0, 0), pipeline_mode=pl.Buffered(1))
    tok3 = pl.BlockSpec((tm, 8, LANES), lambda i: (i, 0, 0))
    row = pl.BlockSpec((tm, npick), lambda i: (i, 0))
    coef = pl.pallas_call(
        functools.partial(_peer_u_kernel, npick=npick),
        grid=(nblk,),
        in_specs=[smem_blk, tab_spec, tok3, row],
        out_specs=row,
        out_shape=jax.ShapeDtypeStruct((t, npick), F32),
        scratch_shapes=[pltpu.VMEM((npick, LANES), F32), pltpu.VMEM((tm, npick), F32)],
        compiler_params=_params(("arbitrary",)),
        name="peer_u",
    )(idx, pu_packed, hn.reshape(t, 8, LANES), gate)
    y3 = pl.pallas_call(
        functools.partial(_peer_v_kernel, npick=npick),
        grid=(nblk,),
        in_specs=[smem_blk, smem_blk, tab_spec, tok3],
        out_specs=tok3,
        out_shape=jax.ShapeDtypeStruct((t, 8, LANES), F32),
        compiler_params=_params(("arbitrary",)),
        name="peer_v",
    )(idx, coef.reshape(nblk, 1, tm * npick), pv_packed, hres.reshape(t, 8, LANES))
    return y3.reshape(t, d)


def _final_norm_kernel(y_ref, g_ref, o_ref):
    y = y_ref[...]
    o_ref[...] = y * lax.rsqrt(jnp.mean(y * y, axis=-1, keepdims=True) + EPS) * g_ref[...]


def _final_norm(y2, g, tm):
    t, d = y2.shape
    return pl.pallas_call(
        _final_norm_kernel,
        grid=(t // tm,),
        in_specs=[pl.BlockSpec((tm, d), lambda i: (i, 0)), pl.BlockSpec((1, d), lambda i: (0, 0))],
        out_specs=pl.BlockSpec((tm, d), lambda i: (i, 0)),
        out_shape=jax.ShapeDtypeStruct((t, d), F32),
        compiler_params=_params(("parallel",)),
        name="final_norm",
    )(y2, g.reshape(1, d))


def _row_tile(t, cap):
    tm = min(t, cap)
    assert t % tm == 0
    return tm


def _layer(x, mstate, past, lw, nb, pu_packed, pv_packed):
    (norm_mix, w_in, b_i, b_f, g_hn, w_out, norm_ffn, w_pq, subkeys, rel_bias) = lw
    batch, l, d = x.shape
    t = batch * l
    assert t % KEY_TILE == 0
    x2 = x.reshape(t, d)
    (mq, mk, mv, mo, aq, ak, akb, av, iq, ik, ikd, gates, vt3) = _inproj(x2, norm_mix, w_in, _row_tile(t, 256))
    c0, n0, m0 = mstate
    lc = CHUNK if past is None else l
    hm, c_new, n_new, m_new = _mlstm(mq, mk, mv, mo, gates, b_i, b_f, g_hn, c0, n0, m0, batch, lc)
    if past is None:
        assert l % KEY_TILE == 0
        ha = _dsa(rel_bias, nb, aq, iq, gates, akb, vt3, ikd, batch=batch, nq=KEY_TILE, q_off=0,
                  lk=l, ltot=l, topk=min(TOPK_MAX, l // 4))
    else:
        pk, pv, pki = past
        plen = pk.shape[1]
        ltot = plen + l
        lk = -(-ltot // KEY_TILE) * KEY_TILE
        pad = ((0, 0), (0, lk - ltot), (0, 0))
        k_all = jnp.pad(jnp.concatenate([pk.reshape(batch, plen, D_ATT).astype(BF16),
                                         akb.reshape(batch, l, D_ATT)], axis=1), pad)
        v_all = jnp.pad(jnp.concatenate([pv.reshape(batch, plen, D_ATT),
                                         av.reshape(batch, l, D_ATT)], axis=1).astype(BF16), pad)
        ki_all = jnp.pad(jnp.concatenate([pki, ik.reshape(batch, l, D_IDX)], axis=1).astype(BF16), pad)
        vt_all = v_all.reshape(batch, lk // KEY_TILE, KEY_TILE, D_ATT).transpose(0, 1, 3, 2)
        ha = _dsa(rel_bias, nb[..., :l], aq, iq, gates, k_all.reshape(batch * lk, D_ATT),
                  vt_all.reshape(batch * (lk // KEY_TILE), D_ATT, KEY_TILE),
                  jnp.concatenate([ki_all, ki_all], axis=-1).reshape(batch * lk, LANES),
                  batch=batch, nq=l, q_off=plen, lk=lk, ltot=ltot, topk=min(TOPK_MAX, ltot // 4))
    hres, hn, eidx_t, gate_t = _route(x2, hm, ha, w_out, norm_ffn, w_pq, subkeys, _row_tile(t, 256))
    y2 = _peer(hres, hn, eidx_t, gate_t, pu_packed, pv_packed, _row_tile(t, 128))
    new_kv = (ak.reshape(batch, l, H_A, DH_A), av.reshape(batch, l, H_A, DH_A), ik.reshape(batch, l, D_IDX))
    return y2.reshape(batch, l, d), new_kv, (c_new, n_new, m_new)


def kernel(x_prompt, x_sample, cache_k, cache_v, cache_kidx, state_C, state_n, state_m, norm_mix, w_in, b_igate, b_fgate, g_headnorm, w_out, norm_ffn, w_peer_q, peer_subkeys, peer_u, peer_v, rel_bias, norm_final):
    depth = w_in.shape[0]
    y_p, y_s = x_prompt, x_sample
    nb = _bias_tiles(rel_bias)
    outs_p, outs_s = [], []
    for l in range(depth):
        lw = (norm_mix[l], w_in[l], b_igate[l], b_fgate[l], g_headnorm[l], w_out[l],
              norm_ffn[l], w_peer_q[l], peer_subkeys[l], rel_bias)
        pu_packed = _pack_table(peer_u[l])
        pv_packed = _pack_table(peer_v[l])
        b = y_p.shape[0]
        zero = (jnp.zeros((b, H_M, DH_M, DH_M), F32), jnp.zeros((b, H_M, DH_M), F32), jnp.zeros((b, H_M), F32))
        y_p, kv_p, st_p = _layer(y_p, zero, None, lw, nb, pu_packed, pv_packed)
        outs_p.append(kv_p + tuple(s.astype(r.dtype) for s, r in zip(st_p, (state_C, state_n, state_m))))
        mst = (state_C[l].astype(F32), state_n[l].astype(F32), state_m[l].astype(F32))
        y_s, kv_s, st_s = _layer(y_s, mst, (cache_k[l], cache_v[l], cache_kidx[l]), lw, nb, pu_packed, pv_packed)
        outs_s.append(kv_s + tuple(s.astype(r.dtype) for s, r in zip(st_s, (state_C, state_n, state_m))))

    def fin(y):
        bb, ll, d = y.shape
        return _final_norm(y.reshape(bb * ll, d), norm_final, _row_tile(bb * ll, 512)).reshape(bb, ll, d)

    stack = lambda outs, i: jnp.stack([o[i] for o in outs])
    return ((fin(y_p), fin(y_s)) + tuple(stack(outs_p, i) for i in range(6))
            + tuple(stack(outs_s, i) for i in range(6)))
```

```python
import functools
import math

import jax
import jax.numpy as jnp
from jax import lax
from jax.experimental import pallas as pl
from jax.experimental.pallas import tpu as pltpu

F32 = jnp.float32
BF16 = jnp.bfloat16
I32 = jnp.int32

CHUNK = 64
H_M = 4
DH_M = 128
D_MLSTM = H_M * DH_M
H_A = 8
DH_A = 64
D_ATT = H_A * DH_A
H_IDX = 8
D_IDX = 64
TOPK_MAX = 256
N_BUCKETS = 32
N_KEYS = 128
PEER_HEADS = 8
PEER_TOPK = 16
EPS = 1e-6

LANES = 128
KEY_TILE = 128
DSA_QUERIES = 256
VMEM_LIMIT = 56 * 1024 * 1024
INT_MIN = -(2 ** 31)
QK_SCALE = DH_A ** -0.5
IDX_SCALE = D_IDX ** -0.5
assert QK_SCALE == 0.125 and IDX_SCALE == 0.125

_NT = (((1,), (1,)), ((), ()))
_TN = (((0,), (0,)), ((), ()))


def _nt(a, b):
    return lax.dot_general(a, b, _NT, preferred_element_type=F32)


def _params(sem, vmem=VMEM_LIMIT):
    return pltpu.CompilerParams(dimension_semantics=sem, vmem_limit_bytes=vmem)


def _inproj_kernel(x_ref, g_ref, w_ref, wvt_ref,
                   mq_ref, mk_ref, mv_ref, mo_ref, aq_ref, ak_ref, akb_ref, av_ref,
                   iq_ref, ik_ref, ikd_ref, gates_ref, vt_ref):
    x = x_ref[...]
    xn = x * lax.rsqrt(jnp.mean(x * x, axis=-1, keepdims=True) + EPS) * g_ref[...]
    xb = xn.astype(BF16)

    def proj(lo, hi):
        return jnp.dot(xb, w_ref[:, lo:hi], preferred_element_type=F32)

    mq_ref[...] = proj(0, 512).astype(BF16)
    mk_ref[...] = proj(512, 1024).astype(BF16)
    mv_ref[...] = proj(1024, 1536).astype(BF16)
    mo_ref[...] = proj(1536, 2048)
    aq_ref[...] = proj(2048, 2560).astype(BF16)
    ak = proj(2560, 3072)
    ak_ref[...] = ak
    akb_ref[...] = ak.astype(BF16)
    av_ref[...] = proj(3072, 3584)
    iq_ref[...] = proj(3584, 4096).astype(BF16)
    ikd = proj(4096, 4224)
    ik_ref[...] = ikd[:, :D_IDX]
    ikd_ref[...] = ikd.astype(BF16)
    gates_ref[...] = proj(4224, 4352)
    vt = _nt(wvt_ref[...], xb)
    for i in range(vt.shape[1] // KEY_TILE):
        vt_ref[i] = vt[:, i * KEY_TILE:(i + 1) * KEY_TILE].astype(BF16)


def _split_w_in(w_in):
    sizes = (D_MLSTM, D_MLSTM, D_MLSTM, D_MLSTM, H_M, H_M, D_ATT, D_ATT, D_ATT, H_IDX * D_IDX, D_IDX, H_IDX)
    pts, acc = [], 0
    for s in sizes[:-1]:
        acc += s
        pts.append(acc)
    return jnp.split(w_in, pts, axis=1)


def _inproj(x2, norm_g, w_in, tm):
    t, d = x2.shape
    m_q, m_k, m_v, m_o, m_i, m_f, a_q, a_k, a_v, i_q, i_k, i_w = _split_w_in(w_in)
    gates = jnp.concatenate([m_i, m_f, i_w], axis=1)
    gates = jnp.pad(gates, ((0, 0), (0, LANES - gates.shape[1])))
    w_main = jnp.concatenate([m_q, m_k, m_v, m_o, a_q, a_k, a_v, i_q, i_k, i_k, gates], axis=1).astype(BF16)
    w_vt = a_v.T.astype(BF16)
    nw = w_main.shape[1]
    row = lambda n: pl.BlockSpec((tm, n), lambda i: (i, 0))
    const = lambda shape: pl.BlockSpec(shape, lambda i: (0,) * len(shape))
    sds = jax.ShapeDtypeStruct
    out_shape = (
        sds((t, 512), BF16), sds((t, 512), BF16), sds((t, 512), BF16), sds((t, 512), F32),
        sds((t, 512), BF16), sds((t, 512), F32), sds((t, 512), BF16), sds((t, 512), F32),
        sds((t, 512), BF16), sds((t, D_IDX), F32), sds((t, LANES), BF16), sds((t, LANES), F32),
        sds((t // KEY_TILE, 512, KEY_TILE), BF16),
    )
    out_specs = (
        row(512), row(512), row(512), row(512), row(512), row(512), row(512), row(512),
        row(512), row(D_IDX), row(LANES), row(LANES),
        pl.BlockSpec((tm // KEY_TILE, 512, KEY_TILE), lambda i: (i, 0, 0)),
    )
    return pl.pallas_call(
        _inproj_kernel,
        grid=(t // tm,),
        in_specs=[row(d), const((1, d)), const((d, nw)), const((512, d))],
        out_specs=out_specs,
        out_shape=out_shape,
        compiler_params=_params(("parallel",)),
        name="inproj",
    )(x2, norm_g.reshape(1, d), w_main, w_vt)


def _mlstm_kernel(bi_ref, bf_ref, q_ref, k_ref, v_ref, o_ref, g_ref, ghn_ref, c0_ref, n0_ref, m0_ref,
                  h_ref, cout_ref, nout_ref, mout_ref, c_s, n_s, m_s, *, lc):
    c = pl.program_id(1)

    @pl.when(c == 0)
    def _():
        c_s[...] = c0_ref[0]
        n_s[...] = n0_ref[0]
        m_s[...] = m0_ref[0]

    g = g_ref[...]
    gt = g[:, 0:16].T
    ri = lax.broadcasted_iota(I32, (lc, lc), 0)
    ci = lax.broadcasted_iota(I32, (lc, lc), 1)
    causal = ci <= ri
    scale = DH_M ** -0.5

    for h in range(H_M):
        hs = slice(h * DH_M, (h + 1) * DH_M)
        ig_col = g[:, h:h + 1] + bi_ref[h]
        lf_col = jax.nn.log_sigmoid(g[:, H_M + h:H_M + h + 1] + bf_ref[h])
        ig_row = gt[h:h + 1, :] + bi_ref[h]
        lf_row = jax.nn.log_sigmoid(gt[H_M + h:H_M + h + 1, :] + bf_ref[h])
        bt_col = jnp.sum(jnp.where(causal, lf_row, 0.0), axis=1, keepdims=True)
        bt_row = jnp.sum(jnp.where(ri <= ci, lf_col, 0.0), axis=0, keepdims=True)
        dmat = jnp.where(causal, bt_col - bt_row + ig_row, -jnp.inf)
        m_old = m_s[h:h + 1, 0:1]
        inter = bt_col + m_old
        m_t = jnp.maximum(inter, jnp.max(dmat, axis=1, keepdims=True))
        w_intra = jnp.exp(dmat - m_t)
        w_inter = jnp.exp(inter - m_t)
        qh = q_ref[:, hs]
        kh = k_ref[:, hs]
        vh = v_ref[:, hs]
        s = _nt(qh, kh) * scale * w_intra
        cmat = c_s[h]
        n_row = n_s[h:h + 1, :]
        qc = _nt(qh, cmat.astype(BF16))
        num = jnp.dot(s.astype(BF16), vh, preferred_element_type=F32) + qc * w_inter
        qn = jnp.sum(qh.astype(F32) * n_row, axis=1, keepdims=True)
        nq = jnp.sum(s, axis=1, keepdims=True) + w_inter * qn
        denom = jnp.maximum(jnp.abs(nq), jnp.exp(-m_t))
        hh = num / denom
        m_new = m_t[lc - 1:lc, :]
        bt_last = bt_col[lc - 1:lc, :]
        wk_col = jnp.exp(bt_last - bt_col + ig_col - m_new)
        decay = jnp.exp(bt_last + m_old - m_new)
        wkv = (wk_col * vh.astype(F32)).astype(BF16)
        c_s[h] = decay * cmat + lax.dot_general(wkv, kh, _TN, preferred_element_type=F32) * scale
        n_s[h:h + 1, :] = decay * n_row + jnp.sum(wk_col * kh.astype(F32), axis=0, keepdims=True) * scale
        m_s[h:h + 1, :] = jnp.broadcast_to(m_new, (1, LANES))
        hn = hh * lax.rsqrt(jnp.mean(hh * hh, axis=1, keepdims=True) + EPS) * ghn_ref[:, hs]
        h_ref[:, hs] = (hn * jax.nn.sigmoid(o_ref[:, hs])).astype(BF16)

    @pl.when(c == pl.num_programs(1) - 1)
    def _():
        cout_ref[0] = c_s[...]
        nout_ref[0] = n_s[...]
        mout_ref[0] = m_s[...]


def _mlstm(mq, mk, mv, mo, gates, b_i, b_f, g_hn, c0, n0, m0, batch, lc):
    t = mq.shape[0]
    nc = t // (batch * lc)
    row = lambda n: pl.BlockSpec((lc, n), lambda b, c: (b * nc + c, 0))
    smem = pl.BlockSpec(memory_space=pltpu.SMEM)
    st4 = pl.BlockSpec((1, H_M, DH_M, DH_M), lambda b, c: (b, 0, 0, 0))
    st3 = pl.BlockSpec((1, 8, LANES), lambda b, c: (b, 0, 0))
    n0p = jnp.pad(n0, ((0, 0), (0, 8 - H_M), (0, 0)))
    m0p = jnp.broadcast_to(jnp.pad(m0, ((0, 0), (0, 8 - H_M)))[:, :, None], (batch, 8, LANES))
    sds = jax.ShapeDtypeStruct
    h, c_new, n_new, m_new = pl.pallas_call(
        functools.partial(_mlstm_kernel, lc=lc),
        grid=(batch, nc),
        in_specs=[smem, smem, row(512), row(512), row(512), row(512), row(LANES),
                  pl.BlockSpec((1, 512), lambda b, c: (0, 0)), st4, st3, st3],
        out_specs=(row(512), st4, st3, st3),
        out_shape=(sds((t, 512), BF16), sds((batch, H_M, DH_M, DH_M), F32),
                   sds((batch, 8, LANES), F32), sds((batch, 8, LANES), F32)),
        scratch_shapes=[pltpu.VMEM((H_M, DH_M, DH_M), F32), pltpu.VMEM((8, LANES), F32),
                        pltpu.VMEM((8, LANES), F32)],
        compiler_params=_params(("parallel", "arbitrary")),
        name="mlstm",
    )(b_i, b_f, mq, mk, mv, mo, gates, g_hn.reshape(1, 512), c0, n0p, m0p)
    return h, c_new, n_new[:, :H_M, :], m_new[:, :H_M, 0]


def _bias_kernel(rb_ref, nb_ref):
    kk = lax.broadcasted_iota(I32, (KEY_TILE, LANES), 0)
    qq = lax.broadcasted_iota(I32, (KEY_TILE, LANES), 1)
    nb = N_BUCKETS // 2
    max_exact = nb // 2
    for d in range(2):
        rel = kk - qq - KEY_TILE * d
        n = jnp.abs(rel)
        e = (pltpu.bitcast((n * n).astype(F32), I32) >> 23) - 127
        large = jnp.minimum(max_exact + (e - 6), nb - 1)
        bucket = jnp.where(rel > 0, nb, 0) + jnp.where(n < max_exact, n, large)
        for h in range(H_A):
            acc = jnp.zeros((KEY_TILE, LANES), F32)
            for b in range(N_BUCKETS):
                acc = jnp.where(bucket == b, rb_ref[b, h], acc)
            nb_ref[h, d] = acc - rb_ref[nb - 1, h]


def _bias_tiles(rel_bias):
    return pl.pallas_call(
        _bias_kernel,
        in_specs=[pl.BlockSpec(memory_space=pltpu.SMEM)],
        out_shape=jax.ShapeDtypeStruct((H_A, 2, KEY_TILE, LANES), F32),
        name="bias_tiles",
    )(rel_bias)


def _fold(x, op):
    return op(x.reshape(KEY_TILE // 8, 8, x.shape[-1]), axis=0)


def _dsa_kernel(q_ref, qi_ref, g_ref, k_ref, vt_ref, ki_ref, nb_ref, o_ref,
                keys_s, mb_s, lg_s, qm_s, qim_s, acc_s, *, nq, q_off, ltot, topk):
    j = pl.program_id(1)
    qbase = j * nq + q_off
    nqt = max(nq // LANES, 1)
    tdiag = (qbase + nq - 1) // KEY_TILE
    nt = tdiag + 1
    qpos = qbase + lax.broadcasted_iota(I32, (1, nq), 1)
    limit_q = jnp.minimum((qpos // CHUNK + 1) * CHUNK, ltot)
    rowi = lax.broadcasted_iota(I32, (KEY_TILE, nq), 0)
    w_t = g_ref[:, 8:16].T * (H_IDX ** -0.5)
    lo_half = lax.broadcasted_iota(I32, (LANES, nq), 0) < DH_A

    def tile_rows(t):
        return pl.ds(pl.multiple_of(t * KEY_TILE, KEY_TILE), KEY_TILE)

    q_t = q_ref[...].astype(F32).T * QK_SCALE
    qi_t = qi_ref[...].astype(F32).T * IDX_SCALE
    for h in range(H_A):
        keep = lo_half if h % 2 == 0 else jnp.logical_not(lo_half)
        sl = slice((h // 2) * LANES, (h // 2 + 1) * LANES)
        qm_s[h] = jnp.where(keep, q_t[sl, :], 0.0).astype(BF16)
        qim_s[h] = jnp.where(keep, qi_t[sl, :], 0.0).astype(BF16)

    def score_tile(t, carry):
        kid = ki_ref[tile_rows(t), :]
        isc = jnp.zeros((KEY_TILE, nq), F32)
        for h in range(H_IDX):
            r = jnp.dot(kid, qim_s[h], preferred_element_type=F32)
            isc = isc + w_t[h:h + 1, :] * jnp.maximum(r, 0.0)
        isc = jnp.where(isc == 0.0, 0.0, isc)
        adm = (t * KEY_TILE + rowi) < limit_q
        isc = jnp.where(adm, isc, -jnp.inf)
        bits = pltpu.bitcast(isc, I32)
        keys_s[t] = jnp.where(bits < 0, bits ^ 0x7FFFFFFF, bits)
        return carry

    lax.fori_loop(0, nt, score_tile, 0)

    def count(pred, cand):
        def body(t, acc):
            return acc + _fold(jnp.where(pred(keys_s[t], cand), 1, 0), jnp.sum)
        acc = lax.fori_loop(0, nt, body, jnp.zeros((8, nq), I32))
        return jnp.sum(acc, axis=0, keepdims=True)

    ge = lambda a, b: a >= b
    gt = lambda a, b: a > b
    zero = jnp.zeros((1, nq), I32)
    cur = jnp.where(count(ge, zero) >= topk, zero, jnp.full((1, nq), INT_MIN, I32))

    def bit_body(i, cur):
        cand = cur + jnp.left_shift(jnp.int32(1), 30 - i)
        return jnp.where(count(ge, cand) >= topk, cand, cur)

    thr = lax.fori_loop(0, 31, bit_body, cur)
    n_gt = count(gt, thr)
    n_tie = count(ge, thr) - n_gt
    need = topk - n_gt
    cut_ties = jnp.max(jnp.where(n_tie > jnp.maximum(need, 0), 1, 0)) > 0

    @pl.when(jnp.logical_not(cut_ties))
    def _():
        def select_tile(t, carry):
            adm = (t * KEY_TILE + rowi) < limit_q
            mb_s[t] = jnp.where(adm, jnp.where(keys_s[t] >= thr, 0.0, -jnp.inf), -jnp.inf)
            return carry
        lax.fori_loop(0, nt, select_tile, 0)

    @pl.when(cut_ties)
    def _():
        tri = (lax.broadcasted_iota(I32, (KEY_TILE, KEY_TILE), 1)
               <= lax.broadcasted_iota(I32, (KEY_TILE, KEY_TILE), 0)).astype(BF16)
        need_f = need.astype(F32)

        def select_tile(t, before):
            key = keys_s[t]
            tie = jnp.where(key == thr, 1.0, 0.0)
            rank = jnp.dot(tri, tie.astype(BF16), preferred_element_type=F32) + before
            take = jnp.where(key > thr, 1.0, jnp.where(rank <= need_f, tie, 0.0))
            adm = (t * KEY_TILE + rowi) < limit_q
            mb_s[t] = jnp.where(adm, jnp.where(take > 0.0, 0.0, -jnp.inf), -jnp.inf)
            return before + jnp.sum(tie, axis=0, keepdims=True)

        lax.fori_loop(0, nt, select_tile, jnp.zeros((1, nq), F32))

    def logit_tile(t, mx, bias):
        rows = tile_rows(t)
        mb = mb_s[t]
        out = []
        for h in range(H_A):
            kt = k_ref[rows, (h // 2) * LANES:(h // 2 + 1) * LANES]
            lg = (jnp.dot(kt, qm_s[h], preferred_element_type=F32)
                  + (mb if bias is None else mb + bias(h)))
            lg_s[h, t] = lg
            out.append(jnp.maximum(mx[h], _fold(lg, jnp.max)))
        return tuple(out)

    mx = tuple(jnp.full((8, nq), -jnp.inf, F32) for _ in range(H_A))
    mx = lax.fori_loop(0, jnp.maximum(tdiag - nqt, 0), lambda t, c: logit_tile(t, c, None), mx)
    for i in range(nqt + 1):
        t = tdiag - nqt + i
        absent = jnp.where(t >= 0, 0.0, -jnp.inf)

        def near_bias(h, i=i, absent=absent):
            parts = []
            for c in range(nqt):
                d = c + 1 - i
                parts.append(nb_ref[h, d] if d in (0, 1) else jnp.zeros((KEY_TILE, nb_ref.shape[-1]), F32))
            return (parts[0] if nqt == 1 else jnp.concatenate(parts, axis=1)) + absent

        mx = logit_tile(jnp.maximum(t, 0), mx, near_bias)
    m = [jnp.max(mx[h], axis=0, keepdims=True) for h in range(H_A)]

    acc_s[...] = jnp.zeros_like(acc_s)

    def pv_tile(t, ls):
        out = []
        for h in range(H_A):
            p = jnp.exp(lg_s[h, t] - m[h])
            hs = slice(h * DH_A, (h + 1) * DH_A)
            acc_s[hs, :] += jnp.dot(vt_ref[t, hs, :], p.astype(BF16), preferred_element_type=F32)
            out.append(ls[h] + _fold(p, jnp.sum))
        return tuple(out)

    ls = lax.fori_loop(0, nt, pv_tile, tuple(jnp.zeros((8, nq), F32) for _ in range(H_A)))
    for h in range(H_A):
        hs = slice(h * DH_A, (h + 1) * DH_A)
        acc_s[hs, :] = acc_s[hs, :] / jnp.sum(ls[h], axis=0, keepdims=True)
    o_ref[...] = acc_s[...].T.astype(BF16)


def _dsa(nb, q, qi, gates, k, vt3, kid, *, batch, nq, q_off, lk, ltot, topk):
    t = q.shape[0]
    nqb = t // (batch * nq)
    ntile = lk // KEY_TILE
    qrow = lambda n: pl.BlockSpec((nq, n), lambda b, j: (b * nqb + j, 0))
    tiles = lambda dt: pltpu.VMEM((ntile, KEY_TILE, nq), dt)
    return pl.pallas_call(
        functools.partial(_dsa_kernel, nq=nq, q_off=q_off, ltot=ltot, topk=topk),
        grid=(batch, nqb),
        in_specs=[qrow(512), qrow(512), qrow(LANES),
                  pl.BlockSpec((lk, 512), lambda b, j: (b, 0)),
                  pl.BlockSpec((ntile, 512, KEY_TILE), lambda b, j: (b, 0, 0)),
                  pl.BlockSpec((lk, LANES), lambda b, j: (b, 0)),
                  pl.BlockSpec((H_A, 2, KEY_TILE, min(nq, LANES)), lambda b, j: (0, 0, 0, 0))],
        out_specs=qrow(512),
        out_shape=jax.ShapeDtypeStruct((t, 512), BF16),
        scratch_shapes=[tiles(I32), tiles(F32), pltpu.VMEM((H_A, ntile, KEY_TILE, nq), F32),
                        pltpu.VMEM((H_A, LANES, nq), BF16), pltpu.VMEM((H_IDX, LANES, nq), BF16),
                        pltpu.VMEM((D_ATT, nq), F32)],
        compiler_params=_params(("parallel", "arbitrary")),
        name="dsa",
    )(q, qi, gates, k, vt3, kid, nb)


def _topk_rows(s, k, n):
    rows = lax.broadcasted_iota(I32, s.shape, 0)
    vals, idxs = [], []
    for _ in range(k):
        m = jnp.max(s, axis=0, keepdims=True)
        first = jnp.min(jnp.where(s == m, rows, n), axis=0, keepdims=True)
        vals.append(m)
        idxs.append(first)
        s = jnp.where(rows == first, -jnp.inf, s)
    return jnp.concatenate(vals, axis=0), jnp.concatenate(idxs, axis=0)


def _candidate_pairs():
    pairs = [(a, b) for a in range(PEER_TOPK) for b in range(PEER_TOPK) if (a + 1) * (b + 1) <= PEER_TOPK]
    nrow = -(-len(pairs) // 8) * 8
    sel = [[[1.0 if r < len(pairs) and pairs[r][side] == c else 0.0 for c in range(PEER_TOPK)]
            for r in range(nrow)] for side in range(2)]
    return jnp.asarray(sel, F32), len(pairs)


def _pick(sel, x):
    return jnp.dot(sel, x, precision=lax.Precision.HIGHEST, preferred_element_type=F32)


def _route_kernel(x_ref, hm_ref, ha_ref, wo_ref, g_ref, wpq_ref, sk_ref, sel_ref,
                  h_ref, hn_ref, eidx_ref, gate_ref, qp_s, *, npair):
    tm = x_ref.shape[0]
    h = (x_ref[...]
         + jnp.dot(hm_ref[...], wo_ref[0:D_MLSTM, :], preferred_element_type=F32)
         + jnp.dot(ha_ref[...], wo_ref[D_MLSTM:, :], preferred_element_type=F32))
    h_ref[...] = h
    hn = h * lax.rsqrt(jnp.mean(h * h, axis=-1, keepdims=True) + EPS) * g_ref[...]
    hn_ref[...] = hn
    qp = jnp.dot(hn.astype(BF16), wpq_ref[...], preferred_element_type=F32)
    for grp in range(2 * PEER_HEADS):
        qp_s[grp] = qp[:, grp * N_KEYS:(grp + 1) * N_KEYS].astype(BF16)

    def head_body(hh, carry):
        s1 = _nt(sk_ref[2 * hh], qp_s[2 * hh])
        s2 = _nt(sk_ref[2 * hh + 1], qp_s[2 * hh + 1])
        v1, i1 = _topk_rows(s1, PEER_TOPK, N_KEYS)
        v2, i2 = _topk_rows(s2, PEER_TOPK, N_KEYS)
        sel_a = sel_ref[0]
        sel_b = sel_ref[1]
        rows = lax.broadcasted_iota(I32, (sel_a.shape[0], tm), 0)
        cand = jnp.where(rows < npair, _pick(sel_a, v1) + _pick(sel_b, v2), -jnp.inf)
        cidx = (_pick(sel_a, i1.astype(F32)) * N_KEYS + _pick(sel_b, i2.astype(F32))).astype(I32)
        tops, eids = [], []
        for _ in range(PEER_TOPK):
            m = jnp.max(cand, axis=0, keepdims=True)
            first = jnp.min(jnp.where(cand == m, rows, npair), axis=0, keepdims=True)
            hit = rows == first
            tops.append(m)
            eids.append(jnp.sum(jnp.where(hit, cidx, 0), axis=0, keepdims=True))
            cand = jnp.where(hit, -jnp.inf, cand)
        top_s = jnp.concatenate(tops, axis=0)
        ex = jnp.exp(top_s - top_s[0:1, :])
        off = pl.multiple_of(hh * PEER_TOPK, PEER_TOPK)
        gate_ref[pl.ds(off, PEER_TOPK), :] = ex / jnp.sum(ex, axis=0, keepdims=True)
        eidx_ref[pl.ds(off, PEER_TOPK), :] = jnp.concatenate(eids, axis=0)
        return carry

    lax.fori_loop(0, PEER_HEADS, head_body, 0)


def _route(x2, hm, ha, w_out, norm_g, w_pq, subkeys, tm):
    t, d = x2.shape
    npq = w_pq.shape[1]
    row = lambda n: pl.BlockSpec((tm, n), lambda i: (i, 0))
    col = lambda n: pl.BlockSpec((n, tm), lambda i: (0, i))
    const = lambda shape: pl.BlockSpec(shape, lambda i: (0,) * len(shape))
    sds = jax.ShapeDtypeStruct
    npick = PEER_HEADS * PEER_TOPK
    sel, npair = _candidate_pairs()
    return pl.pallas_call(
        functools.partial(_route_kernel, npair=npair),
        grid=(t // tm,),
        in_specs=[row(d), row(D_MLSTM), row(D_ATT), const((D_MLSTM + D_ATT, d)), const((1, d)),
                  const((d, npq)), const((2 * PEER_HEADS, N_KEYS, N_KEYS)), const(sel.shape)],
        out_specs=(row(d), row(d), col(npick), col(npick)),
        out_shape=(sds((t, d), F32), sds((t, d), F32), sds((npick, t), I32), sds((npick, t), F32)),
        scratch_shapes=[pltpu.VMEM((2 * PEER_HEADS, tm, N_KEYS), BF16)],
        compiler_params=_params(("parallel",)),
        name="route",
    )(x2, hm, ha, w_out.astype(BF16), norm_g.reshape(1, d), w_pq.astype(BF16),
      subkeys.reshape(2 * PEER_HEADS, N_KEYS, N_KEYS).astype(BF16), sel)


def _pack_table(tab):
    e, d = tab.shape
    b = tab.astype(BF16).reshape(e, d // (2 * LANES), 2, LANES)
    return lax.bitcast_convert_type(jnp.swapaxes(b, -1, -2), jnp.uint32).reshape(e * ROW_WORDS, LANES)


def _gelu_tanh(x):
    return 0.5 * x * (1.0 + jnp.tanh(math.sqrt(2.0 / math.pi) * (x + 0.044715 * (x * x * x))))


def _gather_rows(idx_ref, tab_ref, slab_ref, base, npick):
    ids = idx_ref.at[0, 0, pl.ds(base, npick)]
    for i in range(npick // 2):
        ra = tab_ref[pl.ds(ids[2 * i], ROW_WORDS), :]
        rb = tab_ref[pl.ds(ids[2 * i + 1], ROW_WORDS), :]
        slab_ref[8 * i:8 * i + 8, :] = jnp.concatenate([ra, rb], axis=0)


def _split_bf16(x):
    hi = x.astype(BF16)
    lo = (x - hi.astype(F32)).astype(BF16)
    return jnp.concatenate([hi, lo], axis=0)


ROW_WORDS = 4
PEER_UNROLL = 4
PEER_POST = 16


def _diag8(ncol):
    return (lax.broadcasted_iota(I32, (8, ncol), 1) & 7) == lax.broadcasted_iota(I32, (8, ncol), 0)


def _peer_u_kernel(idx_ref, tab_ref, x_ref, gate_ref, grp_ref, c_ref, slab_s, r_s, *, npick):
    tm = x_ref.shape[0]
    ncol = 8 * npick

    def tokens(i, carry):
        for u in range(PEER_UNROLL):
            t = i * PEER_UNROLL + u
            _gather_rows(idx_ref, tab_ref, slab_s.at[u], t * npick, npick)
            rows = pltpu.bitcast(slab_s[u], BF16)
            r_s[t] = _nt(x_ref[t], rows)
        return carry

    lax.fori_loop(0, tm // PEER_UNROLL, tokens, 0)

    diag = _diag8(ncol)[None]
    grp = grp_ref[...]

    def post(i, carry):
        sl = pl.ds(pl.multiple_of(i * PEER_POST, PEER_POST), PEER_POST)
        rm = jnp.where(diag, r_s[sl], 0.0).reshape(PEER_POST * 8, ncol)
        a8 = jnp.dot(_split_bf16(rm), grp, preferred_element_type=F32)
        a8 = a8[:PEER_POST * 8] + a8[PEER_POST * 8:]
        a = jnp.sum(a8.reshape(PEER_POST, 8, npick), axis=1)
        c_ref[sl, :] = gate_ref[sl, :] * _gelu_tanh(a)
        return carry

    lax.fori_loop(0, tm // PEER_POST, post, 0)


def _peer_v_kernel(idx_ref, tab_ref, c_ref, h_ref, exp_ref, y_ref, slab_s, ce_s, *, npick):
    tm = h_ref.shape[0]
    ncol = 8 * npick
    ce = jnp.dot(_split_bf16(c_ref[...]), exp_ref[...], preferred_element_type=F32)
    ce_s[0] = ce[:tm]
    ce_s[1] = ce[tm:]
    diag = _diag8(ncol)

    def tokens(i, carry):
        for u in range(PEER_UNROLL):
            t = i * PEER_UNROLL + u
            _gather_rows(idx_ref, tab_ref, slab_s.at[u], t * npick, npick)
            rows = pltpu.bitcast(slab_s[u], BF16)
            hi = jnp.where(diag, jnp.broadcast_to(ce_s[0, pl.ds(t, 1), :], (8, ncol)), 0.0)
            lo = jnp.where(diag, jnp.broadcast_to(ce_s[1, pl.ds(t, 1), :], (8, ncol)), 0.0)
            cm = jnp.concatenate([hi, lo], axis=0).astype(BF16)
            out = jnp.dot(cm, rows, preferred_element_type=F32)
            y_ref[t] = h_ref[t] + (out[0:8] + out[8:16])
        return carry

    lax.fori_loop(0, tm // PEER_UNROLL, tokens, 0)


def _peer(hres, hn, eidx_t, gate_t, pu_packed, pv_packed, tm):
    t, d = hres.shape
    npick = eidx_t.shape[0]
    nblk = t // tm
    ncol = 8 * npick
    idx = (eidx_t.T * ROW_WORDS).reshape(nblk, 1, tm * npick)
    gate = gate_t.T
    grp = jnp.repeat(jnp.eye(npick, dtype=BF16), 8, axis=0)
    smem_blk = pl.BlockSpec((1, 1, tm * npick), lambda i: (i, 0, 0), memory_space=pltpu.SMEM)
    tab_spec = pl.BlockSpec(pu_packed.shape, lambda i: (0, 0), pipeline_mode=pl.Buffered(1))
    tok3 = pl.BlockSpec((tm, 8, LANES), lambda i: (i, 0, 0))
    row = pl.BlockSpec((tm, npick), lambda i: (i, 0))
    const2 = lambda shape: pl.BlockSpec(shape, lambda i: (0, 0))
    slab = pltpu.VMEM((PEER_UNROLL, 4 * npick, LANES), jnp.uint32)
    coef = pl.pallas_call(
        functools.partial(_peer_u_kernel, npick=npick),
        grid=(nblk,),
        in_specs=[smem_blk, tab_spec, tok3, row, const2((ncol, npick))],
        out_specs=row,
        out_shape=jax.ShapeDtypeStruct((t, npick), F32),
        scratch_shapes=[slab, pltpu.VMEM((tm, 8, ncol), F32)],
        compiler_params=_params(("arbitrary",)),
        name="peer_u",
    )(idx, pu_packed, hn.astype(BF16).reshape(t, 8, LANES), gate, grp)
    y3 = pl.pallas_call(
        functools.partial(_peer_v_kernel, npick=npick),
        grid=(nblk,),
        in_specs=[smem_blk, tab_spec, row, tok3, const2((npick, ncol))],
        out_specs=tok3,
        out_shape=jax.ShapeDtypeStruct((t, 8, LANES), F32),
        scratch_shapes=[slab, pltpu.VMEM((2, tm, ncol), F32)],
        compiler_params=_params(("arbitrary",)),
        name="peer_v",
    )(idx, pv_packed, coef, hres.reshape(t, 8, LANES), grp.T)
    return y3.reshape(t, d)


def _final_norm_kernel(y_ref, g_ref, o_ref):
    y = y_ref[...]
    o_ref[...] = y * lax.rsqrt(jnp.mean(y * y, axis=-1, keepdims=True) + EPS) * g_ref[...]


def _final_norm(y2, g, tm):
    t, d = y2.shape
    return pl.pallas_call(
        _final_norm_kernel,
        grid=(t // tm,),
        in_specs=[pl.BlockSpec((tm, d), lambda i: (i, 0)), pl.BlockSpec((1, d), lambda i: (0, 0))],
        out_specs=pl.BlockSpec((tm, d), lambda i: (i, 0)),
        out_shape=jax.ShapeDtypeStruct((t, d), F32),
        compiler_params=_params(("parallel",)),
        name="final_norm",
    )(y2, g.reshape(1, d))


def _row_tile(t, cap):
    tm = min(t, cap)
    assert t % tm == 0
    return tm


def _layer(x, mstate, past, lw, nb, pu_packed, pv_packed):
    (norm_mix, w_in, b_i, b_f, g_hn, w_out, norm_ffn, w_pq, subkeys, rel_bias) = lw
    batch, l, d = x.shape
    t = batch * l
    assert t % KEY_TILE == 0
    x2 = x.reshape(t, d)
    (mq, mk, mv, mo, aq, ak, akb, av, iq, ik, ikd, gates, vt3) = _inproj(x2, norm_mix, w_in, _row_tile(t, 256))
    c0, n0, m0 = mstate
    lc = CHUNK if past is None else l
    hm, c_new, n_new, m_new = _mlstm(mq, mk, mv, mo, gates, b_i, b_f, g_hn, c0, n0, m0, batch, lc)
    if past is None:
        assert l % KEY_TILE == 0
        nq = DSA_QUERIES if l % DSA_QUERIES == 0 else KEY_TILE
        ha = _dsa(nb, aq, iq, gates, akb, vt3, ikd, batch=batch, nq=nq, q_off=0,
                  lk=l, ltot=l, topk=min(TOPK_MAX, l // 4))
    else:
        pk, pv, pki = past
        plen = pk.shape[1]
        ltot = plen + l
        lk = -(-ltot // KEY_TILE) * KEY_TILE
        pad = ((0, 0), (0, lk - ltot), (0, 0))
        k_all = jnp.pad(jnp.concatenate([pk.reshape(batch, plen, D_ATT).astype(BF16),
                                         akb.reshape(batch, l, D_ATT)], axis=1), pad)
        v_all = jnp.pad(jnp.concatenate([pv.reshape(batch, plen, D_ATT),
                                         av.reshape(batch, l, D_ATT)], axis=1).astype(BF16), pad)
        ki_all = jnp.pad(jnp.concatenate([pki, ik.reshape(batch, l, D_IDX)], axis=1).astype(BF16), pad)
        vt_all = v_all.reshape(batch, lk // KEY_TILE, KEY_TILE, D_ATT).transpose(0, 1, 3, 2)
        ha = _dsa(nb[..., :l], aq, iq, gates, k_all.reshape(batch * lk, D_ATT),
                  vt_all.reshape(batch * (lk // KEY_TILE), D_ATT, KEY_TILE),
                  jnp.concatenate([ki_all, ki_all], axis=-1).reshape(batch * lk, LANES),
                  batch=batch, nq=l, q_off=plen, lk=lk, ltot=ltot, topk=min(TOPK_MAX, ltot // 4))
    hres, hn, eidx_t, gate_t = _route(x2, hm, ha, w_out, norm_ffn, w_pq, subkeys, _row_tile(t, 256))
    y2 = _peer(hres, hn, eidx_t, gate_t, pu_packed, pv_packed, _row_tile(t, 128))
    new_kv = (ak.reshape(batch, l, H_A, DH_A), av.reshape(batch, l, H_A, DH_A), ik.reshape(batch, l, D_IDX))
    return y2.reshape(batch, l, d), new_kv, (c_new, n_new, m_new)


def kernel(x_prompt, x_sample, cache_k, cache_v, cache_kidx, state_C, state_n, state_m, norm_mix, w_in, b_igate, b_fgate, g_headnorm, w_out, norm_ffn, w_peer_q, peer_subkeys, peer_u, peer_v, rel_bias, norm_final):
    depth = w_in.shape[0]
    y_p, y_s = x_prompt, x_sample
    nb = _bias_tiles(rel_bias)
    outs_p, outs_s = [], []
    for l in range(depth):
        lw = (norm_mix[l], w_in[l], b_igate[l], b_fgate[l], g_headnorm[l], w_out[l],
              norm_ffn[l], w_peer_q[l], peer_subkeys[l], rel_bias)
        pu_packed = _pack_table(peer_u[l])
        pv_packed = _pack_table(peer_v[l])
        b = y_p.shape[0]
        zero = (jnp.zeros((b, H_M, DH_M, DH_M), F32), jnp.zeros((b, H_M, DH_M), F32), jnp.zeros((b, H_M), F32))
        y_p, kv_p, st_p = _layer(y_p, zero, None, lw, nb, pu_packed, pv_packed)
        outs_p.append(kv_p + tuple(s.astype(r.dtype) for s, r in zip(st_p, (state_C, state_n, state_m))))
        mst = (state_C[l].astype(F32), state_n[l].astype(F32), state_m[l].astype(F32))
        y_s, kv_s, st_s = _layer(y_s, mst, (cache_k[l], cache_v[l], cache_kidx[l]), lw, nb, pu_packed, pv_packed)
        outs_s.append(kv_s + tuple(s.astype(r.dtype) for s, r in zip(st_s, (state_C, state_n, state_m))))

    def fin(y):
        bb, ll, d = y.shape
        return _final_norm(y.reshape(bb * ll, d), norm_final, _row_tile(bb * ll, 512)).reshape(bb, ll, d)

    stack = lambda outs, i: jnp.stack([o[i] for o in outs])
    return ((fin(y_p), fin(y_s)) + tuple(stack(outs_p, i) for i in range(6))
            + tuple(stack(outs_s, i) for i in range(6)))
```

```python
import functools
import math

import jax
import jax.numpy as jnp
from jax import lax
from jax.experimental import pallas as pl
from jax.experimental.pallas import tpu as pltpu

F32 = jnp.float32
BF16 = jnp.bfloat16
I32 = jnp.int32

CHUNK = 64
H_M = 4
DH_M = 128
D_MLSTM = H_M * DH_M
H_A = 8
DH_A = 64
D_ATT = H_A * DH_A
H_IDX = 8
D_IDX = 64
TOPK_MAX = 256
N_BUCKETS = 32
N_KEYS = 128
PEER_HEADS = 8
PEER_TOPK = 16
EPS = 1e-6

LANES = 128
KEY_TILE = 128
ROUTE_COLS = 256
MLSTM_SEQS = 2
DSA_QUERIES = 256
VMEM_LIMIT = 56 * 1024 * 1024
INT_MIN = -(2 ** 31)
QK_SCALE = DH_A ** -0.5
IDX_SCALE = D_IDX ** -0.5
assert QK_SCALE == 0.125 and IDX_SCALE == 0.125

_NT = (((1,), (1,)), ((), ()))
_TN = (((0,), (0,)), ((), ()))


def _nt(a, b):
    return lax.dot_general(a, b, _NT, preferred_element_type=F32)


def _params(sem, vmem=VMEM_LIMIT):
    return pltpu.CompilerParams(dimension_semantics=sem, vmem_limit_bytes=vmem)


def _inproj_kernel(x_ref, g_ref, w_ref, wvt_ref,
                   mq_ref, mk_ref, mv_ref, mo_ref, aq_ref, ak_ref, akb_ref, av_ref,
                   iq_ref, ik_ref, ikd_ref, gates_ref, vt_ref):
    x = x_ref[...]
    xn = x * lax.rsqrt(jnp.mean(x * x, axis=-1, keepdims=True) + EPS) * g_ref[...]
    xb = xn.astype(BF16)

    def proj(lo, hi):
        return jnp.dot(xb, w_ref[:, lo:hi], preferred_element_type=F32)

    mq_ref[...] = proj(0, 512).astype(BF16)
    mk_ref[...] = proj(512, 1024).astype(BF16)
    mv_ref[...] = proj(1024, 1536).astype(BF16)
    mo_ref[...] = proj(1536, 2048)
    aq_ref[...] = proj(2048, 2560).astype(BF16)
    ak = proj(2560, 3072)
    ak_ref[...] = ak
    akb_ref[...] = ak.astype(BF16)
    av_ref[...] = proj(3072, 3584)
    iq_ref[...] = proj(3584, 4096).astype(BF16)
    ikd = proj(4096, 4224)
    ik_ref[...] = ikd[:, :D_IDX]
    ikd_ref[...] = ikd.astype(BF16)
    gates_ref[...] = proj(4224, 4352)
    vt = _nt(wvt_ref[...], xb)
    for i in range(vt.shape[1] // KEY_TILE):
        vt_ref[i] = vt[:, i * KEY_TILE:(i + 1) * KEY_TILE].astype(BF16)


def _split_w_in(w_in):
    sizes = (D_MLSTM, D_MLSTM, D_MLSTM, D_MLSTM, H_M, H_M, D_ATT, D_ATT, D_ATT, H_IDX * D_IDX, D_IDX, H_IDX)
    pts, acc = [], 0
    for s in sizes[:-1]:
        acc += s
        pts.append(acc)
    return jnp.split(w_in, pts, axis=1)


def _inproj(x2, norm_g, w_in, tm):
    t, d = x2.shape
    m_q, m_k, m_v, m_o, m_i, m_f, a_q, a_k, a_v, i_q, i_k, i_w = _split_w_in(w_in)
    gates = jnp.concatenate([m_i, m_f, i_w], axis=1)
    gates = jnp.pad(gates, ((0, 0), (0, LANES - gates.shape[1])))
    w_main = jnp.concatenate([m_q, m_k, m_v, m_o, a_q, a_k, a_v, i_q, i_k, i_k, gates], axis=1).astype(BF16)
    w_vt = a_v.T.astype(BF16)
    nw = w_main.shape[1]
    row = lambda n: pl.BlockSpec((tm, n), lambda i: (i, 0))
    const = lambda shape: pl.BlockSpec(shape, lambda i: (0,) * len(shape))
    sds = jax.ShapeDtypeStruct
    out_shape = (
        sds((t, 512), BF16), sds((t, 512), BF16), sds((t, 512), BF16), sds((t, 512), F32),
        sds((t, 512), BF16), sds((t, 512), F32), sds((t, 512), BF16), sds((t, 512), F32),
        sds((t, 512), BF16), sds((t, D_IDX), F32), sds((t, LANES), BF16), sds((t, LANES), F32),
        sds((t // KEY_TILE, 512, KEY_TILE), BF16),
    )
    out_specs = (
        row(512), row(512), row(512), row(512), row(512), row(512), row(512), row(512),
        row(512), row(D_IDX), row(LANES), row(LANES),
        pl.BlockSpec((tm // KEY_TILE, 512, KEY_TILE), lambda i: (i, 0, 0)),
    )
    return pl.pallas_call(
        _inproj_kernel,
        grid=(t // tm,),
        in_specs=[row(d), const((1, d)), const((d, nw)), const((512, d))],
        out_specs=out_specs,
        out_shape=out_shape,
        compiler_params=_params(("parallel",)),
        name="inproj",
    )(x2, norm_g.reshape(1, d), w_main, w_vt)


def _mlstm_kernel(bi_ref, bf_ref, q_ref, k_ref, v_ref, o_ref, g_ref, ghn_ref, c0_ref, n0_ref, m0_ref,
                  h_ref, cout_ref, nout_ref, mout_ref, c_s, n_s, m_s, *, lc, nseq):
    c = pl.program_id(1)

    @pl.when(c == 0)
    def _():
        c_s[...] = c0_ref[...]
        n_s[...] = n0_ref[...]
        m_s[...] = m0_ref[...]

    ri = lax.broadcasted_iota(I32, (lc, lc), 0)
    ci = lax.broadcasted_iota(I32, (lc, lc), 1)
    causal = ci <= ri
    scale = DH_M ** -0.5

    for b in range(nseq):
        g = g_ref[b]
        gt = g[:, 0:16].T
        for h in range(H_M):
            hs = slice(h * DH_M, (h + 1) * DH_M)
            ig_col = g[:, h:h + 1] + bi_ref[h]
            lf_col = jax.nn.log_sigmoid(g[:, H_M + h:H_M + h + 1] + bf_ref[h])
            ig_row = gt[h:h + 1, :] + bi_ref[h]
            lf_row = jax.nn.log_sigmoid(gt[H_M + h:H_M + h + 1, :] + bf_ref[h])
            bt_col = jnp.sum(jnp.where(causal, lf_row, 0.0), axis=1, keepdims=True)
            bt_row = jnp.sum(jnp.where(ri <= ci, lf_col, 0.0), axis=0, keepdims=True)
            dmat = jnp.where(causal, bt_col - bt_row + ig_row, -jnp.inf)
            m_old = m_s[b, h:h + 1, 0:1]
            inter = bt_col + m_old
            m_t = jnp.maximum(inter, jnp.max(dmat, axis=1, keepdims=True))
            w_intra = jnp.exp(dmat - m_t)
            w_inter = jnp.exp(inter - m_t)
            qh = q_ref[b, :, hs]
            kh = k_ref[b, :, hs]
            vh = v_ref[b, :, hs]
            s = _nt(qh, kh) * scale * w_intra
            cmat = c_s[b, h]
            n_row = n_s[b, h:h + 1, :]
            qc = _nt(qh, cmat.astype(BF16))
            num = jnp.dot(s.astype(BF16), vh, preferred_element_type=F32) + qc * w_inter
            qn = jnp.sum(qh.astype(F32) * n_row, axis=1, keepdims=True)
            nq = jnp.sum(s, axis=1, keepdims=True) + w_inter * qn
            denom = jnp.maximum(jnp.abs(nq), jnp.exp(-m_t))
            hh = num / denom
            m_new = m_t[lc - 1:lc, :]
            bt_last = bt_col[lc - 1:lc, :]
            wk_col = jnp.exp(bt_last - bt_col + ig_col - m_new)
            decay = jnp.exp(bt_last + m_old - m_new)
            wkv = (wk_col * vh.astype(F32)).astype(BF16)
            c_s[b, h] = decay * cmat + lax.dot_general(wkv, kh, _TN, preferred_element_type=F32) * scale
            n_s[b, h:h + 1, :] = (decay * n_row
                                  + jnp.sum(wk_col * kh.astype(F32), axis=0, keepdims=True) * scale)
            m_s[b, h:h + 1, :] = jnp.broadcast_to(m_new, (1, LANES))
            hn = hh * lax.rsqrt(jnp.mean(hh * hh, axis=1, keepdims=True) + EPS) * ghn_ref[:, hs]
            h_ref[b, :, hs] = (hn * jax.nn.sigmoid(o_ref[b, :, hs])).astype(BF16)

    @pl.when(c == pl.num_programs(1) - 1)
    def _():
        cout_ref[...] = c_s[...]
        nout_ref[...] = n_s[...]
        mout_ref[...] = m_s[...]


def _mlstm(mq, mk, mv, mo, gates, b_i, b_f, g_hn, c0, n0, m0, batch, lc):
    t = mq.shape[0]
    seq = t // batch
    nc = seq // lc
    nseq = MLSTM_SEQS if batch % MLSTM_SEQS == 0 else 1
    row = lambda n: pl.BlockSpec((nseq, lc, n), lambda b, c: (b, c, 0))
    smem = pl.BlockSpec(memory_space=pltpu.SMEM)
    st4 = pl.BlockSpec((nseq, H_M, DH_M, DH_M), lambda b, c: (b, 0, 0, 0))
    st3 = pl.BlockSpec((nseq, 8, LANES), lambda b, c: (b, 0, 0))
    n0p = jnp.pad(n0, ((0, 0), (0, 8 - H_M), (0, 0)))
    m0p = jnp.broadcast_to(jnp.pad(m0, ((0, 0), (0, 8 - H_M)))[:, :, None], (batch, 8, LANES))
    sds = jax.ShapeDtypeStruct
    per_seq = lambda a: a.reshape(batch, seq, a.shape[-1])
    h, c_new, n_new, m_new = pl.pallas_call(
        functools.partial(_mlstm_kernel, lc=lc, nseq=nseq),
        grid=(batch // nseq, nc),
        in_specs=[smem, smem, row(512), row(512), row(512), row(512), row(LANES),
                  pl.BlockSpec((1, 512), lambda b, c: (0, 0)), st4, st3, st3],
        out_specs=(row(512), st4, st3, st3),
        out_shape=(sds((batch, seq, 512), BF16), sds((batch, H_M, DH_M, DH_M), F32),
                   sds((batch, 8, LANES), F32), sds((batch, 8, LANES), F32)),
        scratch_shapes=[pltpu.VMEM((nseq, H_M, DH_M, DH_M), F32), pltpu.VMEM((nseq, 8, LANES), F32),
                        pltpu.VMEM((nseq, 8, LANES), F32)],
        compiler_params=_params(("parallel", "arbitrary")),
        name="mlstm",
    )(b_i, b_f, per_seq(mq), per_seq(mk), per_seq(mv), per_seq(mo), per_seq(gates), g_hn.reshape(1, 512),
      c0, n0p, m0p)
    return h.reshape(t, 512), c_new, n_new[:, :H_M, :], m_new[:, :H_M, 0]


def _bias_kernel(rb_ref, nb_ref):
    kk = lax.broadcasted_iota(I32, (KEY_TILE, LANES), 0)
    qq = lax.broadcasted_iota(I32, (KEY_TILE, LANES), 1)
    nb = N_BUCKETS // 2
    max_exact = nb // 2
    for d in range(2):
        rel = kk - qq - KEY_TILE * d
        n = jnp.abs(rel)
        e = (pltpu.bitcast((n * n).astype(F32), I32) >> 23) - 127
        large = jnp.minimum(max_exact + (e - 6), nb - 1)
        bucket = jnp.where(rel > 0, nb, 0) + jnp.where(n < max_exact, n, large)
        for h in range(H_A):
            acc = jnp.zeros((KEY_TILE, LANES), F32)
            for b in range(N_BUCKETS):
                acc = jnp.where(bucket == b, rb_ref[b, h], acc)
            nb_ref[h, d] = acc - rb_ref[nb - 1, h]


def _bias_tiles(rel_bias):
    return pl.pallas_call(
        _bias_kernel,
        in_specs=[pl.BlockSpec(memory_space=pltpu.SMEM)],
        out_shape=jax.ShapeDtypeStruct((H_A, 2, KEY_TILE, LANES), F32),
        name="bias_tiles",
    )(rel_bias)


def _fold(x, op):
    return op(x.reshape(KEY_TILE // 8, 8, x.shape[-1]), axis=0)


def _dsa_kernel(q_ref, qi_ref, g_ref, k_ref, vt_ref, ki_ref, nb_ref, o_ref,
                keys_s, mb_s, lg_s, qm_s, qim_s, acc_s, *, nq, q_off, ltot, topk):
    j = pl.program_id(1)
    qbase = j * nq + q_off
    nqt = max(nq // LANES, 1)
    tdiag = (qbase + nq - 1) // KEY_TILE
    nt = tdiag + 1
    qpos = qbase + lax.broadcasted_iota(I32, (1, nq), 1)
    limit_q = jnp.minimum((qpos // CHUNK + 1) * CHUNK, ltot)
    rowi = lax.broadcasted_iota(I32, (KEY_TILE, nq), 0)
    w_t = g_ref[:, 8:16].T * (H_IDX ** -0.5)
    lo_half = lax.broadcasted_iota(I32, (LANES, nq), 0) < DH_A

    def tile_rows(t):
        return pl.ds(pl.multiple_of(t * KEY_TILE, KEY_TILE), KEY_TILE)

    q_t = q_ref[...].astype(F32).T * QK_SCALE
    qi_t = qi_ref[...].astype(F32).T * IDX_SCALE
    for h in range(H_A):
        keep = lo_half if h % 2 == 0 else jnp.logical_not(lo_half)
        sl = slice((h // 2) * LANES, (h // 2 + 1) * LANES)
        qm_s[h] = jnp.where(keep, q_t[sl, :], 0.0).astype(BF16)
        qim_s[h] = jnp.where(keep, qi_t[sl, :], 0.0).astype(BF16)

    def score_tile(t, carry):
        kid = ki_ref[tile_rows(t), :]
        isc = jnp.zeros((KEY_TILE, nq), F32)
        for h in range(H_IDX):
            r = jnp.dot(kid, qim_s[h], preferred_element_type=F32)
            isc = isc + w_t[h:h + 1, :] * jnp.maximum(r, 0.0)
        isc = jnp.where(isc == 0.0, 0.0, isc)
        adm = (t * KEY_TILE + rowi) < limit_q
        isc = jnp.where(adm, isc, -jnp.inf)
        bits = pltpu.bitcast(isc, I32)
        keys_s[t] = jnp.where(bits < 0, bits ^ 0x7FFFFFFF, bits)
        return carry

    lax.fori_loop(0, nt, score_tile, 0)

    def count(pred, cand):
        def body(t, acc):
            return acc + _fold(jnp.where(pred(keys_s[t], cand), 1, 0), jnp.sum)
        acc = lax.fori_loop(0, nt, body, jnp.zeros((8, nq), I32))
        return jnp.sum(acc, axis=0, keepdims=True)

    ge = lambda a, b: a >= b
    gt = lambda a, b: a > b
    zero = jnp.zeros((1, nq), I32)
    cur = jnp.where(count(ge, zero) >= topk, zero, jnp.full((1, nq), INT_MIN, I32))

    def bit_body(i, cur):
        cand = cur + jnp.left_shift(jnp.int32(1), 30 - i)
        return jnp.where(count(ge, cand) >= topk, cand, cur)

    thr = lax.fori_loop(0, 31, bit_body, cur)
    n_gt = count(gt, thr)
    n_tie = count(ge, thr) - n_gt
    need = topk - n_gt
    cut_ties = jnp.max(jnp.where(n_tie > jnp.maximum(need, 0), 1, 0)) > 0

    @pl.when(jnp.logical_not(cut_ties))
    def _():
        def select_tile(t, carry):
            adm = (t * KEY_TILE + rowi) < limit_q
            mb_s[t] = jnp.where(adm, jnp.where(keys_s[t] >= thr, 0.0, -jnp.inf), -jnp.inf)
            return carry
        lax.fori_loop(0, nt, select_tile, 0)

    @pl.when(cut_ties)
    def _():
        tri = (lax.broadcasted_iota(I32, (KEY_TILE, KEY_TILE), 1)
               <= lax.broadcasted_iota(I32, (KEY_TILE, KEY_TILE), 0)).astype(BF16)
        need_f = need.astype(F32)

        def select_tile(t, before):
            key = keys_s[t]
            tie = jnp.where(key == thr, 1.0, 0.0)
            rank = jnp.dot(tri, tie.astype(BF16), preferred_element_type=F32) + before
            take = jnp.where(key > thr, 1.0, jnp.where(rank <= need_f, tie, 0.0))
            adm = (t * KEY_TILE + rowi) < limit_q
            mb_s[t] = jnp.where(adm, jnp.where(take > 0.0, 0.0, -jnp.inf), -jnp.inf)
            return before + jnp.sum(tie, axis=0, keepdims=True)

        lax.fori_loop(0, nt, select_tile, jnp.zeros((1, nq), F32))

    def logit_tile(t, mx, bias):
        rows = tile_rows(t)
        mb = mb_s[t]
        out = []
        for h in range(H_A):
            kt = k_ref[rows, (h // 2) * LANES:(h // 2 + 1) * LANES]
            lg = (jnp.dot(kt, qm_s[h], preferred_element_type=F32)
                  + (mb if bias is None else mb + bias(h)))
            lg_s[h, t] = lg
            out.append(jnp.maximum(mx[h], _fold(lg, jnp.max)))
        return tuple(out)

    mx = tuple(jnp.full((8, nq), -jnp.inf, F32) for _ in range(H_A))
    mx = lax.fori_loop(0, jnp.maximum(tdiag - nqt, 0), lambda t, c: logit_tile(t, c, None), mx)
    for i in range(nqt + 1):
        t = tdiag - nqt + i
        absent = jnp.where(t >= 0, 0.0, -jnp.inf)

        def near_bias(h, i=i, absent=absent):
            parts = []
            for c in range(nqt):
                d = c + 1 - i
                parts.append(nb_ref[h, d] if d in (0, 1) else jnp.zeros((KEY_TILE, nb_ref.shape[-1]), F32))
            return (parts[0] if nqt == 1 else jnp.concatenate(parts, axis=1)) + absent

        mx = logit_tile(jnp.maximum(t, 0), mx, near_bias)
    m = [jnp.max(mx[h], axis=0, keepdims=True) for h in range(H_A)]

    acc_s[...] = jnp.zeros_like(acc_s)

    def pv_tile(t, ls):
        out = []
        for h in range(H_A):
            p = jnp.exp(lg_s[h, t] - m[h])
            hs = slice(h * DH_A, (h + 1) * DH_A)
            acc_s[hs, :] += jnp.dot(vt_ref[t, hs, :], p.astype(BF16), preferred_element_type=F32)
            out.append(ls[h] + _fold(p, jnp.sum))
        return tuple(out)

    ls = lax.fori_loop(0, nt, pv_tile, tuple(jnp.zeros((8, nq), F32) for _ in range(H_A)))
    for h in range(H_A):
        hs = slice(h * DH_A, (h + 1) * DH_A)
        acc_s[hs, :] = acc_s[hs, :] / jnp.sum(ls[h], axis=0, keepdims=True)
    o_ref[...] = acc_s[...].T.astype(BF16)


def _dsa(nb, q, qi, gates, k, vt3, kid, *, batch, nq, q_off, lk, ltot, topk):
    t = q.shape[0]
    nqb = t // (batch * nq)
    ntile = lk // KEY_TILE
    qrow = lambda n: pl.BlockSpec((nq, n), lambda b, j: (b * nqb + j, 0))
    tiles = lambda dt: pltpu.VMEM((ntile, KEY_TILE, nq), dt)
    return pl.pallas_call(
        functools.partial(_dsa_kernel, nq=nq, q_off=q_off, ltot=ltot, topk=topk),
        grid=(batch, nqb),
        in_specs=[qrow(512), qrow(512), qrow(LANES),
                  pl.BlockSpec((lk, 512), lambda b, j: (b, 0)),
                  pl.BlockSpec((ntile, 512, KEY_TILE), lambda b, j: (b, 0, 0)),
                  pl.BlockSpec((lk, LANES), lambda b, j: (b, 0)),
                  pl.BlockSpec((H_A, 2, KEY_TILE, min(nq, LANES)), lambda b, j: (0, 0, 0, 0))],
        out_specs=qrow(512),
        out_shape=jax.ShapeDtypeStruct((t, 512), BF16),
        scratch_shapes=[tiles(I32), tiles(F32), pltpu.VMEM((H_A, ntile, KEY_TILE, nq), F32),
                        pltpu.VMEM((H_A, LANES, nq), BF16), pltpu.VMEM((H_IDX, LANES, nq), BF16),
                        pltpu.VMEM((D_ATT, nq), F32)],
        compiler_params=_params(("parallel", "arbitrary")),
        name="dsa",
    )(q, qi, gates, k, vt3, kid, nb)


def _topk_rows(s, k, n):
    rows = lax.broadcasted_iota(I32, s.shape, 0).astype(F32)
    vals, idxs = [], []
    for _ in range(k):
        m = jnp.max(s, axis=0, keepdims=True)
        first = jnp.min(jnp.where(s == m, rows, float(n)), axis=0, keepdims=True)
        vals.append(m)
        idxs.append(first)
        s = jnp.where(rows == first, -jnp.inf, s)
    return jnp.concatenate(vals, axis=0), jnp.concatenate(idxs, axis=0)


def _candidate_pairs():
    pairs = [(a, b) for a in range(PEER_TOPK) for b in range(PEER_TOPK) if (a + 1) * (b + 1) <= PEER_TOPK]
    nrow = -(-len(pairs) // 8) * 8
    sel = [[[1.0 if r < len(pairs) and pairs[r][side] == c else 0.0 for c in range(PEER_TOPK)]
            for r in range(nrow)] for side in range(2)]
    return jnp.asarray(sel, F32), len(pairs)


def _pick(sel, x):
    return jnp.dot(sel, x, precision=lax.Precision.HIGHEST, preferred_element_type=F32)


def _route_kernel(x_ref, hm_ref, ha_ref, wo_ref, g_ref, wpq_ref, sk_ref, sel_ref,
                  h_ref, hn_ref, rows_ref, gate_ref, qp_s, eid_s, gat_s, *, npair):
    tm = x_ref.shape[0]
    h = (x_ref[...]
         + jnp.dot(hm_ref[...], wo_ref[0:D_MLSTM, :], preferred_element_type=F32)
         + jnp.dot(ha_ref[...], wo_ref[D_MLSTM:, :], preferred_element_type=F32))
    h_ref[...] = h
    hn = h * lax.rsqrt(jnp.mean(h * h, axis=-1, keepdims=True) + EPS) * g_ref[...]
    hnb = hn.astype(BF16)
    hn_ref[...] = hnb
    qp = jnp.dot(hnb, wpq_ref[...], preferred_element_type=F32)
    for grp in range(2 * PEER_HEADS):
        qp_s[grp] = qp[:, grp * N_KEYS:(grp + 1) * N_KEYS].astype(BF16)

    sel_a = sel_ref[0]
    sel_b = sel_ref[1]
    wcol = min(tm, ROUTE_COLS)
    rows = lax.broadcasted_iota(I32, (sel_a.shape[0], wcol), 0).astype(F32)

    def head_body(it, carry):
        hh = it // (tm // wcol)
        cb = it % (tm // wcol)
        col = pl.ds(pl.multiple_of(cb * wcol, wcol), wcol)
        s1 = _nt(sk_ref[2 * hh], qp_s[2 * hh, col, :])
        s2 = _nt(sk_ref[2 * hh + 1], qp_s[2 * hh + 1, col, :])
        v1, i1 = _topk_rows(s1, PEER_TOPK, N_KEYS)
        v2, i2 = _topk_rows(s2, PEER_TOPK, N_KEYS)
        cand = jnp.where(rows < npair, _pick(sel_a, v1) + _pick(sel_b, v2), -jnp.inf)
        cidx = _pick(sel_a, i1) * N_KEYS + _pick(sel_b, i2)
        tops, eids = [], []
        for _ in range(PEER_TOPK):
            m = jnp.max(cand, axis=0, keepdims=True)
            first = jnp.min(jnp.where(cand == m, rows, float(npair)), axis=0, keepdims=True)
            hit = rows == first
            tops.append(m)
            eids.append(jnp.sum(jnp.where(hit, cidx, 0.0), axis=0, keepdims=True))
            cand = jnp.where(hit, -jnp.inf, cand)
        top_s = jnp.concatenate(tops, axis=0)
        ex = jnp.exp(top_s - top_s[0:1, :])
        off = pl.ds(pl.multiple_of(hh * PEER_TOPK, PEER_TOPK), PEER_TOPK)
        gat_s[cb, off, :] = ex / jnp.sum(ex, axis=0, keepdims=True)
        eid_s[cb, off, :] = jnp.concatenate(eids, axis=0)
        return carry

    lax.fori_loop(0, PEER_HEADS * (tm // wcol), head_body, 0)
    for cb in range(tm // wcol):
        tok = slice(cb * wcol, (cb + 1) * wcol)
        gate_ref[tok, :] = gat_s[cb].T
        rows_ref[tok, :] = (eid_s[cb].T * ROW_WORDS).astype(I32)


def _route(x2, hm, ha, w_out, norm_g, w_pq, subkeys, tm):
    t, d = x2.shape
    npq = w_pq.shape[1]
    row = lambda n: pl.BlockSpec((tm, n), lambda i: (i, 0))
    const = lambda shape: pl.BlockSpec(shape, lambda i: (0,) * len(shape))
    sds = jax.ShapeDtypeStruct
    npick = PEER_HEADS * PEER_TOPK
    sel, npair = _candidate_pairs()
    return pl.pallas_call(
        functools.partial(_route_kernel, npair=npair),
        grid=(t // tm,),
        in_specs=[row(d), row(D_MLSTM), row(D_ATT), const((D_MLSTM + D_ATT, d)), const((1, d)),
                  const((d, npq)), const((2 * PEER_HEADS, N_KEYS, N_KEYS)), const(sel.shape)],
        out_specs=(row(d), row(d), row(npick), row(npick)),
        out_shape=(sds((t, d), F32), sds((t, d), BF16), sds((t, npick), I32), sds((t, npick), F32)),
        scratch_shapes=[pltpu.VMEM((2 * PEER_HEADS, tm, N_KEYS), BF16),
                        pltpu.VMEM((-(-tm // ROUTE_COLS), npick, min(tm, ROUTE_COLS)), F32),
                        pltpu.VMEM((-(-tm // ROUTE_COLS), npick, min(tm, ROUTE_COLS)), F32)],
        compiler_params=_params(("parallel",)),
        name="route",
    )(x2, hm, ha, w_out.astype(BF16), norm_g.reshape(1, d), w_pq.astype(BF16),
      subkeys.reshape(2 * PEER_HEADS, N_KEYS, N_KEYS).astype(BF16), sel)


def _pack_kernel(t_ref, o_ref):
    o_ref[...] = pltpu.bitcast(t_ref[...].astype(BF16), jnp.uint32)


def _pack_table(tab):
    e, d = tab.shape
    rows = e * d // LANES
    blk = 8192
    assert rows % blk == 0
    return pl.pallas_call(
        _pack_kernel,
        grid=(rows // blk,),
        in_specs=[pl.BlockSpec((blk, LANES), lambda i: (i, 0))],
        out_specs=pl.BlockSpec((blk // 2, LANES), lambda i: (i, 0)),
        out_shape=jax.ShapeDtypeStruct((rows // 2, LANES), jnp.uint32),
        compiler_params=_params(("parallel",)),
        name="pack_table",
    )(tab.reshape(rows, LANES))


def _gelu_tanh(x):
    return 0.5 * x * (1.0 + jnp.tanh(math.sqrt(2.0 / math.pi) * (x + 0.044715 * (x * x * x))))


def _gather_rows(idx_ref, tab_ref, slab_ref, base, npick):
    ids = idx_ref.at[0, 0, pl.ds(base, npick)]
    for i in range(npick // 2):
        ra = tab_ref[pl.ds(ids[2 * i], ROW_WORDS), :]
        rb = tab_ref[pl.ds(ids[2 * i + 1], ROW_WORDS), :]
        slab_ref[8 * i:8 * i + 8, :] = jnp.concatenate([ra, rb], axis=0)


def _split_bf16(x):
    hi = x.astype(BF16)
    lo = (x - hi.astype(F32)).astype(BF16)
    return jnp.concatenate([hi, lo], axis=0)


ROW_WORDS = 4
PEER_UNROLL = 8
PEER_POST = 16


def _diag8(ncol):
    return (lax.broadcasted_iota(I32, (8, ncol), 1) & 7) == lax.broadcasted_iota(I32, (8, ncol), 0)


def _peer_u_kernel(idx_ref, tab_ref, x_ref, gate_ref, grp_ref, c_ref, slab_s, r_s, *, npick):
    tm = x_ref.shape[0]
    ncol = 8 * npick

    def tokens(i, carry):
        for u in range(PEER_UNROLL):
            t = i * PEER_UNROLL + u
            _gather_rows(idx_ref, tab_ref, slab_s.at[u], t * npick, npick)
            rows = pltpu.bitcast(slab_s[u], BF16)
            r_s[t] = _nt(x_ref[t], rows)
        return carry

    lax.fori_loop(0, tm // PEER_UNROLL, tokens, 0)

    diag = _diag8(ncol)[None]
    grp = grp_ref[...]

    def post(i, carry):
        sl = pl.ds(pl.multiple_of(i * PEER_POST, PEER_POST), PEER_POST)
        rm = jnp.where(diag, r_s[sl], 0.0).reshape(PEER_POST * 8, ncol)
        a8 = jnp.dot(_split_bf16(rm), grp, preferred_element_type=F32)
        a8 = a8[:PEER_POST * 8] + a8[PEER_POST * 8:]
        a = jnp.sum(a8.reshape(PEER_POST, 8, npick), axis=1)
        c_ref[sl, :] = gate_ref[sl, :] * _gelu_tanh(a)
        return carry

    lax.fori_loop(0, tm // PEER_POST, post, 0)


def _peer_v_kernel(idx_ref, tab_ref, c_ref, h_ref, exp_ref, gfin_ref, y_ref, slab_s, ce_s, *, npick, final):
    tm = h_ref.shape[0]
    ncol = 8 * npick
    ce = jnp.dot(_split_bf16(c_ref[...]), exp_ref[...], preferred_element_type=F32)
    ce_s[0] = ce[:tm]
    ce_s[1] = ce[tm:]
    diag = _diag8(ncol)

    def tokens(i, carry):
        for u in range(PEER_UNROLL):
            t = i * PEER_UNROLL + u
            _gather_rows(idx_ref, tab_ref, slab_s.at[u], t * npick, npick)
            rows = pltpu.bitcast(slab_s[u], BF16)
            hi = jnp.where(diag, jnp.broadcast_to(ce_s[0, pl.ds(t, 1), :], (8, ncol)), 0.0)
            lo = jnp.where(diag, jnp.broadcast_to(ce_s[1, pl.ds(t, 1), :], (8, ncol)), 0.0)
            cm = jnp.concatenate([hi, lo], axis=0).astype(BF16)
            out = jnp.dot(cm, rows, preferred_element_type=F32)
            y_ref[t] = h_ref[t] + (out[0:8] + out[8:16])
        return carry

    lax.fori_loop(0, tm // PEER_UNROLL, tokens, 0)
    if final:
        y = y_ref[...]
        ms = jnp.sum(jnp.sum(y * y, axis=2, keepdims=True), axis=1, keepdims=True) * (1.0 / (8 * LANES))
        y_ref[...] = y * lax.rsqrt(ms + EPS) * gfin_ref[...][None]


def _peer(hres, hn, rows, gate, pu_packed, pv_packed, final_g, tm):
    t, d = hres.shape
    npick = rows.shape[1]
    nblk = t // tm
    ncol = 8 * npick
    idx = rows.reshape(nblk, 1, tm * npick)
    grp = jnp.repeat(jnp.eye(npick, dtype=BF16), 8, axis=0)
    smem_blk = pl.BlockSpec((1, 1, tm * npick), lambda i: (i, 0, 0), memory_space=pltpu.SMEM)
    tab_spec = pl.BlockSpec(pu_packed.shape, lambda i: (0, 0), pipeline_mode=pl.Buffered(1))
    tok3 = pl.BlockSpec((tm, 8, LANES), lambda i: (i, 0, 0))
    row = pl.BlockSpec((tm, npick), lambda i: (i, 0))
    const2 = lambda shape: pl.BlockSpec(shape, lambda i: (0, 0))
    slab = pltpu.VMEM((PEER_UNROLL, 4 * npick, LANES), jnp.uint32)
    coef = pl.pallas_call(
        functools.partial(_peer_u_kernel, npick=npick),
        grid=(nblk,),
        in_specs=[smem_blk, tab_spec, tok3, row, const2((ncol, npick))],
        out_specs=row,
        out_shape=jax.ShapeDtypeStruct((t, npick), F32),
        scratch_shapes=[slab, pltpu.VMEM((tm, 8, ncol), F32)],
        compiler_params=_params(("arbitrary",)),
        name="peer_u",
    )(idx, pu_packed, hn.reshape(t, 8, LANES), gate, grp)
    y3 = pl.pallas_call(
        functools.partial(_peer_v_kernel, npick=npick, final=final_g is not None),
        grid=(nblk,),
        in_specs=[smem_blk, tab_spec, row, tok3, const2((npick, ncol)), const2((8, LANES))],
        out_specs=tok3,
        out_shape=jax.ShapeDtypeStruct((t, 8, LANES), F32),
        scratch_shapes=[slab, pltpu.VMEM((2, tm, ncol), F32)],
        compiler_params=_params(("arbitrary",)),
        name="peer_v",
    )(idx, pv_packed, coef, hres.reshape(t, 8, LANES), grp.T,
      (jnp.ones((d,), F32) if final_g is None else final_g).reshape(8, LANES))
    return y3.reshape(t, d)


def _row_tile(t, cap):
    tm = min(t, cap)
    assert t % tm == 0
    return tm


def _layer(x, mstate, past, lw, nb, pu_packed, pv_packed, final_g):
    (norm_mix, w_in, b_i, b_f, g_hn, w_out, norm_ffn, w_pq, subkeys, rel_bias) = lw
    batch, l, d = x.shape
    t = batch * l
    assert t % KEY_TILE == 0
    x2 = x.reshape(t, d)
    (mq, mk, mv, mo, aq, ak, akb, av, iq, ik, ikd, gates, vt3) = _inproj(x2, norm_mix, w_in, _row_tile(t, 256))
    c0, n0, m0 = mstate
    lc = CHUNK if past is None else l
    hm, c_new, n_new, m_new = _mlstm(mq, mk, mv, mo, gates, b_i, b_f, g_hn, c0, n0, m0, batch, lc)
    if past is None:
        assert l % KEY_TILE == 0
        nq = DSA_QUERIES if l % DSA_QUERIES == 0 else KEY_TILE
        ha = _dsa(nb, aq, iq, gates, akb, vt3, ikd, batch=batch, nq=nq, q_off=0,
                  lk=l, ltot=l, topk=min(TOPK_MAX, l // 4))
    else:
        pk, pv, pki = past
        plen = pk.shape[1]
        ltot = plen + l
        lk = -(-ltot // KEY_TILE) * KEY_TILE
        pad = ((0, 0), (0, lk - ltot), (0, 0))
        k_all = jnp.pad(jnp.concatenate([pk.reshape(batch, plen, D_ATT).astype(BF16),
                                         akb.reshape(batch, l, D_ATT)], axis=1), pad)
        v_all = jnp.pad(jnp.concatenate([pv.reshape(batch, plen, D_ATT),
                                         av.reshape(batch, l, D_ATT)], axis=1).astype(BF16), pad)
        ki_all = jnp.pad(jnp.concatenate([pki, ik.reshape(batch, l, D_IDX)], axis=1).astype(BF16), pad)
        vt_all = v_all.reshape(batch, lk // KEY_TILE, KEY_TILE, D_ATT).transpose(0, 1, 3, 2)
        ha = _dsa(nb[..., :l], aq, iq, gates, k_all.reshape(batch * lk, D_ATT),
                  vt_all.reshape(batch * (lk // KEY_TILE), D_ATT, KEY_TILE),
                  jnp.concatenate([ki_all, ki_all], axis=-1).reshape(batch * lk, LANES),
                  batch=batch, nq=l, q_off=plen, lk=lk, ltot=ltot, topk=min(TOPK_MAX, ltot // 4))
    hres, hn, rows, gate = _route(x2, hm, ha, w_out, norm_ffn, w_pq, subkeys, _row_tile(t, 256))
    y2 = _peer(hres, hn, rows, gate, pu_packed, pv_packed, final_g, _row_tile(t, 128))
    new_kv = (ak.reshape(batch, l, H_A, DH_A), av.reshape(batch, l, H_A, DH_A), ik.reshape(batch, l, D_IDX))
    return y2.reshape(batch, l, d), new_kv, (c_new, n_new, m_new)


def kernel(x_prompt, x_sample, cache_k, cache_v, cache_kidx, state_C, state_n, state_m, norm_mix, w_in, b_igate, b_fgate, g_headnorm, w_out, norm_ffn, w_peer_q, peer_subkeys, peer_u, peer_v, rel_bias, norm_final):
    depth = w_in.shape[0]
    y_p, y_s = x_prompt, x_sample
    nb = _bias_tiles(rel_bias)
    outs_p, outs_s = [], []
    for l in range(depth):
        lw = (norm_mix[l], w_in[l], b_igate[l], b_fgate[l], g_headnorm[l], w_out[l],
              norm_ffn[l], w_peer_q[l], peer_subkeys[l], rel_bias)
        pu_packed = _pack_table(peer_u[l])
        pv_packed = _pack_table(peer_v[l])
        b = y_p.shape[0]
        zero = (jnp.zeros((b, H_M, DH_M, DH_M), F32), jnp.zeros((b, H_M, DH_M), F32), jnp.zeros((b, H_M), F32))
        final_g = norm_final if l == depth - 1 else None
        y_p, kv_p, st_p = _layer(y_p, zero, None, lw, nb, pu_packed, pv_packed, final_g)
        outs_p.append(kv_p + tuple(s.astype(r.dtype) for s, r in zip(st_p, (state_C, state_n, state_m))))
        mst = (state_C[l].astype(F32), state_n[l].astype(F32), state_m[l].astype(F32))
        y_s, kv_s, st_s = _layer(y_s, mst, (cache_k[l], cache_v[l], cache_kidx[l]), lw, nb, pu_packed, pv_packed,
                                 final_g)
        outs_s.append(kv_s + tuple(s.astype(r.dtype) for s, r in zip(st_s, (state_C, state_n, state_m))))

    stack = lambda outs, i: jnp.stack([o[i] for o in outs])
    return ((y_p, y_s) + tuple(stack(outs_p, i) for i in range(6))
            + tuple(stack(outs_s, i) for i in range(6)))
```
